```python
import jax, jax.numpy as jnp
from jax import lax
import numpy as np

D_MODEL = 1024
BATCH = 8
SEQ = 2048
DEPTH = 4
DEC_BATCH = 32
DEC_SEQ = 8
PAST_LEN = 8192
PAGE_SIZE = 128

HEAD_DIM = 64
D_MIX = D_MODEL
H_FOX = (D_MIX // 2) // HEAD_DIM
H_SB = (D_MIX // 2) // HEAD_DIM
W_FOX = H_FOX * HEAD_DIM
W_SB = H_SB * HEAD_DIM
D_IN = 4 * W_FOX + H_FOX + 4 * W_SB
SPLITS = (W_FOX, 2 * W_FOX, 3 * W_FOX, 4 * W_FOX, 4 * W_FOX + H_FOX,
          4 * W_FOX + H_FOX + W_SB, 4 * W_FOX + H_FOX + 2 * W_SB,
          4 * W_FOX + H_FOX + 3 * W_SB)
Q_BLOCK = 128
LN_EPS = 1e-5
DEEPNORM_ALPHA = (2 * DEPTH) ** 0.25
DEEPNORM_BETA = (8 * DEPTH) ** -0.25

kernel_name = "hymba_fox_stickbreaking_deepnorm_step"


def _layer_norm(x, g, b):
    xf = x.astype(jnp.float32)
    mu = jnp.mean(xf, axis=-1, keepdims=True)
    var = jnp.mean(jnp.square(xf - mu), axis=-1, keepdims=True)
    y = (xf - mu) * lax.rsqrt(var + LN_EPS) * g.astype(jnp.float32) + b.astype(jnp.float32)
    return y.astype(x.dtype)


def _block_len(t):
    return Q_BLOCK if t % Q_BLOCK == 0 else t


def _fox_attention(q, k, v, dq, dk, q_pos, k_pos):
    B, T, H, Dh = q.shape
    qb = _block_len(T)
    nb = T // qb
    scale = Dh ** -0.5
    q_blk = q.reshape(B, nb, qb, H, Dh).transpose(1, 0, 2, 3, 4)
    dq_blk = dq.reshape(B, nb, qb, H).transpose(1, 0, 3, 2)
    pos_blk = q_pos.reshape(nb, qb)
    dk_t = dk.transpose(0, 2, 1)

    def one_block(args):
        qi, dqi, pi = args
        s = jnp.einsum('bqhd,bkhd->bhqk', qi, k).astype(jnp.float32) * scale
        s = s + dqi[..., None] - dk_t[:, :, None, :]
        mask = k_pos[None, :] <= pi[:, None]
        p = jax.nn.softmax(jnp.where(mask, s, -jnp.inf), axis=-1)
        return jnp.einsum('bhqk,bkhd->bqhd', p.astype(v.dtype), v)

    out = lax.map(one_block, (q_blk, dq_blk, pos_blk))
    return out.transpose(1, 0, 2, 3, 4).reshape(B, T, H, Dh)


def _stick_breaking_attention(q, k, v, q_pos, k_pos):
    B, T, H, Dh = q.shape
    qb = _block_len(T)
    nb = T // qb
    scale = Dh ** -0.5
    q_blk = q.reshape(B, nb, qb, H, Dh).transpose(1, 0, 2, 3, 4)
    pos_blk = q_pos.reshape(nb, qb)

    def one_block(args):
        qi, pi = args
        z = jnp.einsum('bqhd,bkhd->bhqk', qi, k).astype(jnp.float32) * scale
        mask = k_pos[None, :] < pi[:, None]
        log_comp = jnp.where(mask, -jax.nn.softplus(z), 0.0)
        suffix = lax.cumsum(log_comp, axis=3, reverse=True) - log_comp
        a = jnp.where(mask, jnp.exp(jax.nn.log_sigmoid(z) + suffix), 0.0)
        return jnp.einsum('bhqk,bkhd->bqhd', a.astype(v.dtype), v)

    out = lax.map(one_block, (q_blk, pos_blk))
    return out.transpose(1, 0, 2, 3, 4).reshape(B, T, H, Dh)


def _mixer_layer(x, w_in, b_f, w_out, ln_g, ln_b, q_pos, k_pos, past):
    B, T, _ = x.shape
    h = jnp.einsum('btd,de->bte', x, w_in)
    qa, ka, va, ga, fa, qs, ks, vs, gs = jnp.split(h, SPLITS, axis=-1)
    qa = qa.reshape(B, T, H_FOX, HEAD_DIM)
    ka = ka.reshape(B, T, H_FOX, HEAD_DIM)
    va = va.reshape(B, T, H_FOX, HEAD_DIM)
    qs = qs.reshape(B, T, H_SB, HEAD_DIM)
    ks = ks.reshape(B, T, H_SB, HEAD_DIM)
    vs = vs.reshape(B, T, H_SB, HEAD_DIM)
    logf = jax.nn.log_sigmoid(fa.astype(jnp.float32) + b_f.astype(jnp.float32))
    new_rows = (ka, va, logf, ks, vs)
    if past is None:
        ka_all, va_all, logf_all, ks_all, vs_all = new_rows
    else:
        pk_a, pv_a, plogf, pk_s, pv_s = past
        ka_all = jnp.concatenate([pk_a.astype(ka.dtype), ka], axis=1)
        va_all = jnp.concatenate([pv_a.astype(va.dtype), va], axis=1)
        logf_all = jnp.concatenate([plogf.astype(jnp.float32), logf], axis=1)
        ks_all = jnp.concatenate([pk_s.astype(ks.dtype), ks], axis=1)
        vs_all = jnp.concatenate([pv_s.astype(vs.dtype), vs], axis=1)
    d_all = lax.cumsum(logf_all, axis=1)
    dq = d_all[:, -T:]
    o_a = _fox_attention(qa, ka_all, va_all, dq, d_all, q_pos, k_pos).reshape(B, T, W_FOX)
    o_s = _stick_breaking_attention(qs, ks_all, vs_all, q_pos, k_pos).reshape(B, T, W_SB)
    o = jnp.concatenate([o_a * jax.nn.silu(ga), o_s * jax.nn.silu(gs)], axis=-1).astype(x.dtype)
    y = jnp.einsum('bte,ed->btd', o, w_out)
    x_new = _layer_norm(DEEPNORM_ALPHA * x + y, ln_g, ln_b)
    return x_new, new_rows


def setup_inputs(seed: int = 0) -> dict:
    key = jax.random.key(seed)
    ks = jax.random.split(key, 14)
    n_pages = PAST_LEN // PAGE_SIZE
    n_used = DEC_BATCH * n_pages
    n_phys = n_used + n_used // 4
    f32 = jnp.float32
    x_prompt = jax.random.normal(ks[0], (BATCH, SEQ, D_MODEL), f32)
    x_sample = jax.random.normal(ks[1], (DEC_BATCH, DEC_SEQ, D_MODEL), f32)
    b_forget = jnp.linspace(1.0, 5.0, H_FOX, dtype=f32)[None, :] + 0.1 * jax.random.normal(ks[2], (DEPTH, H_FOX), f32)
    shape_a = (DEPTH, n_phys, PAGE_SIZE, H_FOX, HEAD_DIM)
    shape_b = (DEPTH, n_phys, PAGE_SIZE, H_SB, HEAD_DIM)
    cache_fox_k = jax.random.normal(ks[3], shape_a, f32)
    cache_fox_v = jax.random.normal(ks[4], shape_a, f32)
    cache_fox_logf = jax.nn.log_sigmoid(
        b_forget[:, None, None, :] + jax.random.normal(ks[5], (DEPTH, n_phys, PAGE_SIZE, H_FOX), f32))
    cache_sb_k = jax.random.normal(ks[6], shape_b, f32)
    cache_sb_v = jax.random.normal(ks[7], shape_b, f32)
    page_table = jax.random.permutation(ks[8], n_phys)[:n_used].reshape(DEC_BATCH, n_pages).astype(jnp.int32)
    w_in = jax.random.normal(ks[9], (DEPTH, D_MODEL, D_IN), f32) * (D_MODEL ** -0.5)
    w_out = jax.random.normal(ks[10], (DEPTH, D_MIX, D_MODEL), f32) * (D_MIX ** -0.5 * DEEPNORM_BETA)
    ln_gain = 1.0 + 0.05 * jax.random.normal(ks[11], (DEPTH, D_MODEL), f32)
    ln_bias = 0.02 * jax.random.normal(ks[12], (DEPTH, D_MODEL), f32)
    return {"x_prompt": x_prompt, "x_sample": x_sample,
            "cache_fox_k": cache_fox_k, "cache_fox_v": cache_fox_v, "cache_fox_logf": cache_fox_logf,
            "cache_sb_k": cache_sb_k, "cache_sb_v": cache_sb_v, "page_table": page_table,
            "w_in": w_in, "b_forget": b_forget, "w_out": w_out,
            "ln_gain": ln_gain, "ln_bias": ln_bias}


def reference(x_prompt, x_sample, cache_fox_k, cache_fox_v, cache_fox_logf, cache_sb_k, cache_sb_v,
              page_table, w_in, b_forget, w_out, ln_gain, ln_bias):
    T = x_prompt.shape[1]
    Bd, Td = x_sample.shape[:2]
    past_len = page_table.shape[1] * PAGE_SIZE
    p_pos = jnp.arange(T, dtype=jnp.int32)
    s_qpos = past_len + jnp.arange(Td, dtype=jnp.int32)
    s_kpos = jnp.arange(past_len + Td, dtype=jnp.int32)

    def gather(cache, l):
        g = cache[l][page_table]
        return g.reshape((Bd, past_len) + g.shape[3:])

    xp, xs = x_prompt, x_sample
    rows_p, rows_s = [], []
    for l in range(DEPTH):
        xp, rp = _mixer_layer(xp, w_in[l], b_forget[l], w_out[l], ln_gain[l], ln_bias[l],
                              p_pos, p_pos, None)
        past = (gather(cache_fox_k, l), gather(cache_fox_v, l), gather(cache_fox_logf, l),
                gather(cache_sb_k, l), gather(cache_sb_v, l))
        xs, rs = _mixer_layer(xs, w_in[l], b_forget[l], w_out[l], ln_gain[l], ln_bias[l],
                              s_qpos, s_kpos, past)
        rows_p.append(rp)
        rows_s.append(rs)

    new_fox_k_p = jnp.stack([r[0] for r in rows_p])
    new_fox_v_p = jnp.stack([r[1] for r in rows_p])
    new_fox_logf_p = jnp.stack([r[2] for r in rows_p])
    new_sb_k_p = jnp.stack([r[3] for r in rows_p])
    new_sb_v_p = jnp.stack([r[4] for r in rows_p])
    new_fox_k_s = jnp.stack([r[0] for r in rows_s])
    new_fox_v_s = jnp.stack([r[1] for r in rows_s])
    new_fox_logf_s = jnp.stack([r[2] for r in rows_s])
    new_sb_k_s = jnp.stack([r[3] for r in rows_s])
    new_sb_v_s = jnp.stack([r[4] for r in rows_s])
    return (xp, xs, new_fox_k_p, new_fox_v_p, new_fox_logf_p, new_sb_k_p, new_sb_v_p,
            new_fox_k_s, new_fox_v_s, new_fox_logf_s, new_sb_k_s, new_sb_v_s)
```

```python
import functools

import jax
import jax.numpy as jnp
from jax import lax
from jax.experimental import pallas as pl
from jax.experimental.pallas import tpu as pltpu

F32 = jnp.float32
BF16 = jnp.bfloat16

HEAD_DIM = 64
N_HEADS = 8
W_GRP = N_HEADS * HEAD_DIM
PAGE = 128
LANES = 128
LN_EPS = 1e-5
NEG = -1e30
VMEM_LIMIT = 56 * 1024 * 1024

T_TILE = 512
K_SUB = 256
PAGES_PER_STEP = 4


def _log_sigmoid(x):
    return jnp.minimum(x, 0.0) - jnp.log(1.0 + jnp.exp(-jnp.abs(x)))


def _silu(g):
    return g * (1.0 / (1.0 + jnp.exp(-g)))


def _cparams(sem):
    return pltpu.CompilerParams(dimension_semantics=sem, vmem_limit_bytes=VMEM_LIMIT)


def _proj_prompt_kernel(x_ref, w_ref, wf_ref, bf_ref,
                        ka_ref, va_ref, ks_ref, vs_ref, lf_ref, kab_ref, ksb_ref,
                        qat_ref, vat_ref, gat_ref, qst_ref, vst_ref, gst_ref, lft_ref):
    xb = x_ref[...].astype(BF16)

    def chunk(c):
        return jnp.dot(xb, w_ref[:, c * W_GRP:(c + 1) * W_GRP], preferred_element_type=F32)

    qat_ref[0] = chunk(0).T.astype(BF16)
    ka = chunk(1)
    ka_ref[...] = ka
    kab_ref[...] = ka.astype(BF16)
    va = chunk(2)
    va_ref[...] = va
    vat_ref[0] = va.T.astype(BF16)
    gat_ref[0] = chunk(3).T
    qst_ref[0] = chunk(4).T.astype(BF16)
    ks = chunk(5)
    ks_ref[...] = ks
    ksb_ref[...] = ks.astype(BF16)
    vs = chunk(6)
    vs_ref[...] = vs
    vst_ref[0] = vs.T.astype(BF16)
    gst_ref[0] = chunk(7).T
    fa = jnp.dot(xb, wf_ref[...], preferred_element_type=F32) + bf_ref[...]
    lf = _log_sigmoid(fa)
    lf_ref[...] = lf[:, :N_HEADS]
    lft_ref[0] = lf.T[:N_HEADS, :]


def _proj_prompt(x, w, wf, bfp):
    m, d = x.shape
    tm = T_TILE
    nt = m // tm
    tok = lambda n, dt: jax.ShapeDtypeStruct((m, n), dt)
    til = lambda r, dt: jax.ShapeDtypeStruct((nt, r, tm), dt)
    tok_spec = lambda n: pl.BlockSpec((tm, n), lambda i: (i, 0))
    til_spec = lambda r: pl.BlockSpec((1, r, tm), lambda i: (i, 0, 0))
    out_shape = ([tok(W_GRP, F32)] * 4 + [tok(N_HEADS, F32)] + [tok(W_GRP, BF16)] * 2
                 + [til(W_GRP, BF16), til(W_GRP, BF16), til(W_GRP, F32)] * 2 + [til(N_HEADS, F32)])
    out_specs = ([tok_spec(W_GRP)] * 4 + [tok_spec(N_HEADS)] + [tok_spec(W_GRP)] * 2
                 + [til_spec(W_GRP)] * 6 + [til_spec(N_HEADS)])
    return pl.pallas_call(
        _proj_prompt_kernel,
        grid=(nt,),
        in_specs=[pl.BlockSpec((tm, d), lambda i: (i, 0)),
                  pl.BlockSpec(w.shape, lambda i: (0, 0)),
                  pl.BlockSpec(wf.shape, lambda i: (0, 0)),
                  pl.BlockSpec(bfp.shape, lambda i: (0, 0))],
        out_specs=out_specs,
        out_shape=out_shape,
        compiler_params=_cparams(("arbitrary",)),
        name="proj_prompt",
    )(x, w, wf, bfp)


def _proj_sample_kernel(x_ref, w_ref, wf_ref, bf_ref,
                        qa_ref, ka_ref, va_ref, ga_ref, qs_ref, ks_ref, vs_ref, gs_ref, lf_ref):
    xb = x_ref[...].astype(BF16)
    outs = (qa_ref, ka_ref, va_ref, ga_ref, qs_ref, ks_ref, vs_ref, gs_ref)
    for c, o_ref in enumerate(outs):
        o_ref[...] = jnp.dot(xb, w_ref[:, c * W_GRP:(c + 1) * W_GRP], preferred_element_type=F32)
    fa = jnp.dot(xb, wf_ref[...], preferred_element_type=F32) + bf_ref[...]
    lf_ref[...] = _log_sigmoid(fa)[:, :N_HEADS]


def _proj_sample(x, w, wf, bfp):
    m, d = x.shape
    full = lambda a: pl.BlockSpec(a.shape, lambda i: (0,) * a.ndim)
    out_shape = [jax.ShapeDtypeStruct((m, W_GRP), F32)] * 8 + [jax.ShapeDtypeStruct((m, N_HEADS), F32)]
    out_specs = [pl.BlockSpec((m, W_GRP), lambda i: (0, 0))] * 8 + [pl.BlockSpec((m, N_HEADS), lambda i: (0, 0))]
    return pl.pallas_call(
        _proj_sample_kernel,
        grid=(1,),
        in_specs=[full(x), full(w), full(wf), full(bfp)],
        out_specs=out_specs,
        out_shape=out_shape,
        compiler_params=_cparams(("arbitrary",)),
        name="proj_sample",
    )(x, w, wf, bfp)


def _dcum_kernel(lft_ref, dt_ref, *, nq):
    x = jnp.concatenate([lft_ref[i] for i in range(nq)], axis=1)
    t = x.shape[1]
    lane = lax.broadcasted_iota(jnp.int32, x.shape, 1)
    s = 1
    while s < t:
        x = x + jnp.where(lane >= s, pltpu.roll(x, s, axis=1), 0.0)
        s *= 2
    for i in range(nq):
        dt_ref[i] = x[:, i * T_TILE:(i + 1) * T_TILE]


def _dcum(lft, nq):
    spec = pl.BlockSpec((nq, N_HEADS, T_TILE), lambda b: (b, 0, 0))
    return pl.pallas_call(
        functools.partial(_dcum_kernel, nq=nq),
        grid=(lft.shape[0] // nq,),
        in_specs=[spec], out_specs=spec,
        out_shape=jax.ShapeDtypeStruct(lft.shape, F32),
        compiler_params=_cparams(("arbitrary",)),
        name="forget_cumsum",
    )(lft)


def _pair_masked_q(q, odd):
    qq = jnp.concatenate([q, q], axis=0)
    half = lax.broadcasted_iota(jnp.int32, qq.shape, 0) // HEAD_DIM
    return jnp.where(half == odd, qq, jnp.zeros_like(qq))


def _pick_row(x, idx):
    sub = lax.broadcasted_iota(jnp.int32, x.shape, 0)
    return jnp.sum(jnp.where(sub == idx, x, 0.0), axis=0, keepdims=True)


def _fox_prompt_kernel(q_ref, k_ref, v_ref, d_ref, o_ref, dk_ref, *, nq):
    tq, tk = T_TILE, K_SUB
    nsub = tq // tk
    h = pl.program_id(1)
    odd = h % 2
    for kt in range(nq):
        for c in range(tq // LANES):
            row = _pick_row(d_ref[kt, :, c * LANES:(c + 1) * LANES], h)
            dk_ref[kt * tq + c * LANES:kt * tq + (c + 1) * LANES, :] = jnp.broadcast_to(row, (LANES, LANES)).T
    row_i = lax.broadcasted_iota(jnp.int32, (tk, tq), 0)
    col_i = lax.broadcasted_iota(jnp.int32, (tk, tq), 1)

    def block(kt, jj, carry, qm, dq, mask):
        m, l, acc = carry
        r0 = pl.multiple_of(kt * tq + jj * tk, tk)
        s = jnp.dot(k_ref[pl.ds(r0, tk), :], qm, preferred_element_type=F32)
        dk = dk_ref[pl.ds(r0, tk), :]
        s = s + (dq - jnp.concatenate([dk] * (tq // LANES), axis=1))
        if mask is not None:
            s = jnp.where(mask, s, NEG)
        m_new = jnp.maximum(m, jnp.max(s, axis=0, keepdims=True))
        alpha = jnp.exp(m - m_new)
        p = jnp.exp(s - m_new)
        l = alpha * l + jnp.sum(p, axis=0, keepdims=True)
        vb = v_ref[kt, :, jj * tk:(jj + 1) * tk]
        acc = alpha * acc + jnp.dot(vb, p.astype(BF16), preferred_element_type=F32)
        return m_new, l, acc

    def q_tile(qi, _):
        qm = _pair_masked_q(q_ref[qi], odd)
        dq = _pick_row(d_ref[qi], h)
        carry = (jnp.full((1, tq), NEG, F32), jnp.zeros((1, tq), F32), jnp.zeros((HEAD_DIM, tq), F32))

        def full_tile(kt, c):
            for jj in range(nsub):
                c = block(kt, jj, c, qm, dq, None)
            return c

        carry = lax.fori_loop(0, qi, full_tile, carry)
        for jj in range(nsub):
            carry = block(qi, jj, carry, qm, dq, (row_i + jj * tk) <= col_i)
        _, l, acc = carry
        o_ref[qi] = acc * (1.0 / l)
        return 0

    lax.fori_loop(0, nq, q_tile, 0)


def _sb_prompt_kernel(q_ref, k_ref, v_ref, o_ref, *, nq):
    tq, tk = T_TILE, K_SUB
    nsub = tq // tk
    odd = pl.program_id(1) % 2
    row_i = lax.broadcasted_iota(jnp.int32, (tk, tq), 0)
    col_i = lax.broadcasted_iota(jnp.int32, (tk, tq), 1)
    ui = lax.broadcasted_iota(jnp.int32, (tk, tk), 0)
    uj = lax.broadcasted_iota(jnp.int32, (tk, tk), 1)
    upper = jnp.where(uj > ui, 1.0, 0.0).astype(BF16)

    def block(kt, jj, carry, qm, mask):
        tot, acc = carry
        r0 = pl.multiple_of(kt * tq + jj * tk, tk)
        z = jnp.dot(k_ref[pl.ds(r0, tk), :], qm, preferred_element_type=F32)
        lg = jnp.log(1.0 + jnp.exp(-jnp.abs(z)))
        sp = jnp.maximum(z, 0.0) + lg
        ls = jnp.minimum(z, 0.0) - lg
        if mask is not None:
            sp = jnp.where(mask, sp, 0.0)
        hi = sp.astype(BF16)
        lo = (sp - hi.astype(F32)).astype(BF16)
        cs = (jnp.dot(upper, hi, preferred_element_type=F32)
              + jnp.dot(upper, lo, preferred_element_type=F32))
        a = jnp.exp(ls - cs - tot)
        if mask is not None:
            a = jnp.where(mask, a, 0.0)
        vb = v_ref[kt, :, jj * tk:(jj + 1) * tk]
        acc = acc + jnp.dot(vb, a.astype(BF16), preferred_element_type=F32)
        tot = tot + jnp.sum(sp, axis=0, keepdims=True)
        return tot, acc

    def q_tile(qi, _):
        qm = _pair_masked_q(q_ref[qi], odd)
        carry = (jnp.zeros((1, tq), F32), jnp.zeros((HEAD_DIM, tq), F32))
        for jj in reversed(range(nsub)):
            carry = block(qi, jj, carry, qm, (row_i + jj * tk) < col_i)

        def full_tile(it, c):
            kt = qi - 1 - it
            for jj in reversed(range(nsub)):
                c = block(kt, jj, c, qm, None)
            return c

        carry = lax.fori_loop(0, qi, full_tile, carry)
        o_ref[qi] = carry[1]
        return 0

    lax.fori_loop(0, nq, q_tile, 0)


def _attn_prompt(kind, qt, kb, vt, dt, batch, nq):
    t = nq * T_TILE
    til = pl.BlockSpec((nq, HEAD_DIM, T_TILE), lambda b, h: (b, h, 0))
    in_specs = [til,
                pl.BlockSpec((t, 2 * HEAD_DIM), lambda b, h: (b, h // 2)),
                til]
    args = [qt, kb, vt]
    scratch = []
    if kind == "fox":
        in_specs.append(pl.BlockSpec((nq, N_HEADS, T_TILE), lambda b, h: (b, 0, 0)))
        args.append(dt)
        scratch.append(pltpu.VMEM((t, LANES), F32))
        body = functools.partial(_fox_prompt_kernel, nq=nq)
    else:
        body = functools.partial(_sb_prompt_kernel, nq=nq)
    return pl.pallas_call(
        body,
        grid=(batch, N_HEADS),
        in_specs=in_specs,
        out_specs=til,
        out_shape=jax.ShapeDtypeStruct(qt.shape, F32),
        scratch_shapes=scratch,
        compiler_params=_cparams(("arbitrary", "arbitrary")),
        name=kind + "_attn_prompt",
    )(*args)


def _outproj_kernel(oa_ref, os_ref, ga_ref, gs_ref, x_ref, wo_ref, lg_ref, lb_ref, y_ref, *, alpha, transposed):
    if transposed:
        ua = (oa_ref[0] * _silu(ga_ref[0])).T.astype(BF16)
        us = (os_ref[0] * _silu(gs_ref[0])).T.astype(BF16)
    else:
        ua = (oa_ref[...] * _silu(ga_ref[...])).astype(BF16)
        us = (os_ref[...] * _silu(gs_ref[...])).astype(BF16)
    y = (jnp.dot(ua, wo_ref[:W_GRP, :], preferred_element_type=F32)
         + jnp.dot(us, wo_ref[W_GRP:, :], preferred_element_type=F32))
    r = alpha * x_ref[...] + y
    mu = jnp.mean(r, axis=-1, keepdims=True)
    c = r - mu
    var = jnp.mean(c * c, axis=-1, keepdims=True)
    y_ref[...] = c * lax.rsqrt(var + LN_EPS) * lg_ref[...] + lb_ref[...]


def _outproj(oa, os_, ga, gs, x, wo, lg, lb, alpha, transposed):
    m, d = x.shape
    if transposed:
        tm = T_TILE
        o_spec = pl.BlockSpec((1, W_GRP, tm), lambda i: (i, 0, 0))
    else:
        tm = m
        o_spec = pl.BlockSpec((tm, W_GRP), lambda i: (i, 0))
    vec = pl.BlockSpec((1, d), lambda i: (0, 0))
    return pl.pallas_call(
        functools.partial(_outproj_kernel, alpha=alpha, transposed=transposed),
        grid=(m // tm,),
        in_specs=[o_spec] * 4 + [pl.BlockSpec((tm, d), lambda i: (i, 0)),
                                 pl.BlockSpec(wo.shape, lambda i: (0, 0)), vec, vec],
        out_specs=pl.BlockSpec((tm, d), lambda i: (i, 0)),
        out_shape=jax.ShapeDtypeStruct((m, d), F32),
        compiler_params=_cparams(("arbitrary",)),
        name="outproj_ln_" + ("prompt" if transposed else "sample"),
    )(oa, os_, ga, gs, x, wo, lg, lb)


def _decode_kernel(pt_ref, qa_ref, qs_ref, kna_ref, vna_ref, kns_ref, vns_ref, lfn_ref, lfnt_ref, *rest,
                   pg, td):
    del pt_ref
    ka_refs, va_refs = rest[0:pg], rest[pg:2 * pg]
    ks_refs, vs_refs = rest[2 * pg:3 * pg], rest[3 * pg:4 * pg]
    lf_refs = rest[4 * pg:5 * pg]
    oa_ref, os_ref = rest[5 * pg], rest[5 * pg + 1]
    m_ref, l_ref, acca_ref, rc_ref, g_ref, tot_ref, accs_ref = rest[5 * pg + 2:]
    rows = N_HEADS * td
    j = pl.program_id(1)
    lane = lax.broadcasted_iota(jnp.int32, (rows, PAGE), 1)
    tok = lax.broadcasted_iota(jnp.int32, (rows, PAGE), 0) % td
    ui = lax.broadcasted_iota(jnp.int32, (PAGE, PAGE), 0)
    uj = lax.broadcasted_iota(jnp.int32, (PAGE, PAGE), 1)
    later = jnp.where(ui > uj, 1.0, 0.0).astype(BF16)

    def scores(q_ref, k_of_head):
        parts = []
        for h in range(N_HEADS):
            parts.append(lax.dot_general(q_ref[0, h].astype(BF16), k_of_head(h).astype(BF16),
                                         (((1,), (1,)), ((), ())), preferred_element_type=F32))
        return jnp.concatenate(parts, axis=0)

    def rep_heads(r):
        return jnp.concatenate([jnp.broadcast_to(r[h:h + 1, :], (td, PAGE)) for h in range(N_HEADS)], axis=0)

    def fox_page(k_of_head, v_of_head, r_keys, mask):
        s = scores(qa_ref, k_of_head) + g_ref[...] + rep_heads(r_keys)
        if mask is not None:
            s = jnp.where(mask, s, NEG)
        m_prev = m_ref[...]
        m_new = jnp.maximum(m_prev, jnp.max(s, axis=1, keepdims=True))
        alpha = jnp.exp(m_prev - m_new)
        p = jnp.exp(s - m_new)
        l_ref[...] = alpha * l_ref[...] + jnp.sum(p, axis=1, keepdims=True)
        m_ref[...] = m_new
        for h in range(N_HEADS):
            sl = slice(h * td, (h + 1) * td)
            pv = jnp.dot(p[sl].astype(BF16), v_of_head(h).astype(BF16), preferred_element_type=F32)
            acca_ref[sl, :] = alpha[sl, :HEAD_DIM] * acca_ref[sl, :] + pv

    def sb_page(k_of_head, v_of_head, mask):
        z = scores(qs_ref, k_of_head)
        lg = jnp.log(1.0 + jnp.exp(-jnp.abs(z)))
        sp = jnp.maximum(z, 0.0) + lg
        ls = jnp.minimum(z, 0.0) - lg
        if mask is not None:
            sp = jnp.where(mask, sp, 0.0)
        hi = sp.astype(BF16)
        lo = (sp - hi.astype(F32)).astype(BF16)
        cs = (jnp.dot(hi, later, preferred_element_type=F32)
              + jnp.dot(lo, later, preferred_element_type=F32))
        a = jnp.exp(ls - cs - tot_ref[...])
        if mask is not None:
            a = jnp.where(mask, a, 0.0)
        tot_ref[...] = tot_ref[...] + jnp.sum(sp, axis=1, keepdims=True)
        for h in range(N_HEADS):
            sl = slice(h * td, (h + 1) * td)
            accs_ref[sl, :] = accs_ref[sl, :] + jnp.dot(a[sl].astype(BF16), v_of_head(h).astype(BF16),
                                                        preferred_element_type=F32)

    @pl.when(j == 0)
    def _():
        m_ref[...] = jnp.full(m_ref.shape, NEG, F32)
        l_ref[...] = jnp.zeros(l_ref.shape, F32)
        acca_ref[...] = jnp.zeros(acca_ref.shape, F32)
        rc_ref[...] = jnp.zeros(rc_ref.shape, F32)
        tot_ref[...] = jnp.zeros(tot_ref.shape, F32)
        accs_ref[...] = jnp.zeros(accs_ref.shape, F32)
        lfn = lfn_ref[0]
        lfnt = lfnt_ref[0]
        sub = lax.broadcasted_iota(jnp.int32, (td, N_HEADS), 0)
        g = jnp.zeros((td, N_HEADS), F32)
        lane_h = lax.broadcasted_iota(jnp.int32, (N_HEADS, PAGE), 1)
        r_new = jnp.zeros((N_HEADS, PAGE), F32)
        for jt in range(td):
            g = g + jnp.where(sub >= jt, lfn[jt:jt + 1, :], 0.0)
            r_new = r_new - jnp.where((lane_h >= jt) & (lane_h < td),
                                      jnp.broadcast_to(lfnt[:, jt:jt + 1], (N_HEADS, PAGE)), 0.0)
        g_ref[...] = jnp.concatenate(
            [jnp.broadcast_to(g[:, h:h + 1], (td, PAGE)) for h in range(N_HEADS)], axis=0)
        pad = jnp.zeros((PAGE - td, HEAD_DIM), F32)
        new_rows = lambda ref: (lambda h: jnp.concatenate([ref[0, h], pad], axis=0))
        fox_page(new_rows(kna_ref), new_rows(vna_ref), r_new, lane <= tok)
        sb_page(new_rows(kns_ref), new_rows(vns_ref), lane < tok)

    page_rows = lambda ref: (lambda h: ref[pl.ds(h, PAGE, stride=N_HEADS), :])
    for i in range(pg):
        lf = lf_refs[i][...]
        x = lf
        lane_h = lax.broadcasted_iota(jnp.int32, lf.shape, 1)
        s = 1
        while s < PAGE:
            x = x + jnp.where(lane_h + s < PAGE, pltpu.roll(x, PAGE - s, axis=1), 0.0)
            s *= 2
        r_keys = (x - lf) + rc_ref[...]
        rc_ref[...] = rc_ref[...] + jnp.sum(lf, axis=1, keepdims=True)
        fox_page(page_rows(ka_refs[i]), page_rows(va_refs[i]), r_keys, None)
        sb_page(page_rows(ks_refs[i]), page_rows(vs_refs[i]), None)

    @pl.when(j == pl.num_programs(1) - 1)
    def _():
        inv = 1.0 / l_ref[...]
        for h in range(N_HEADS):
            sl = slice(h * td, (h + 1) * td)
            oa_ref[0, h] = acca_ref[sl, :] * inv[sl, :HEAD_DIM]
            os_ref[0, h] = accs_ref[sl, :]


def _decode_attn(layer, page_table, qa, qs, kna, vna, kns, vns, lfn, lfnt, cka, cva, cks, cvs, clf):
    bd, n_pages = page_table.shape
    td = qa.shape[2]
    pg = PAGES_PER_STEP
    rows = N_HEADS * td
    small = lambda a: pl.BlockSpec((1,) + a.shape[1:], lambda b, j, pt: (b,) + (0,) * (a.ndim - 1))

    def page_spec(i, shape):
        def imap(b, j, pt):
            return (layer, pt[b, n_pages - 1 - (j * pg + i)], 0, 0)
        return pl.BlockSpec((None, None) + shape, imap)

    kv_shape = (PAGE * N_HEADS, HEAD_DIM)
    in_specs = [small(a) for a in (qa, qs, kna, vna, kns, vns, lfn, lfnt)]
    args = [qa, qs, kna, vna, kns, vns, lfn, lfnt]
    for cache in (cka, cva, cks, cvs):
        in_specs += [page_spec(i, kv_shape) for i in range(pg)]
        args += [cache] * pg
    in_specs += [page_spec(i, (N_HEADS, PAGE)) for i in range(pg)]
    args += [clf] * pg
    out_spec = pl.BlockSpec((1, N_HEADS, td, HEAD_DIM), lambda b, j, pt: (b, 0, 0, 0))
    out_shape = jax.ShapeDtypeStruct((bd, N_HEADS, td, HEAD_DIM), F32)
    stat = pltpu.VMEM((rows, PAGE), F32)
    accum = pltpu.VMEM((rows, HEAD_DIM), F32)
    grid_spec = pltpu.PrefetchScalarGridSpec(
        num_scalar_prefetch=1,
        grid=(bd, n_pages // pg),
        in_specs=in_specs,
        out_specs=[out_spec, out_spec],
        scratch_shapes=[stat, stat, accum, pltpu.VMEM((N_HEADS, PAGE), F32), stat, stat, accum],
    )
    return pl.pallas_call(
        functools.partial(_decode_kernel, pg=pg, td=td),
        grid_spec=grid_spec,
        out_shape=[out_shape, out_shape],
        compiler_params=_cparams(("arbitrary", "arbitrary")),
        name="decode_attn",
    )(page_table, *args)


def kernel(x_prompt, x_sample, cache_fox_k, cache_fox_v, cache_fox_logf, cache_sb_k, cache_sb_v,
           page_table, w_in, b_forget, w_out, ln_gain, ln_bias):
    batch, seq, d_model = x_prompt.shape
    bd, td, _ = x_sample.shape
    depth = w_in.shape[0]
    n_phys = cache_fox_k.shape[1]
    assert seq % T_TILE == 0 and page_table.shape[1] % PAGES_PER_STEP == 0
    assert cache_fox_k.shape[2:] == (PAGE, N_HEADS, HEAD_DIM) and w_in.shape[2] == 8 * W_GRP + N_HEADS
    nq = seq // T_TILE
    alpha = (2 * depth) ** 0.25

    scale = HEAD_DIM ** -0.5
    ga_end = 4 * W_GRP
    wa, wf, wb = w_in[:, :, :ga_end], w_in[:, :, ga_end:ga_end + N_HEADS], w_in[:, :, ga_end + N_HEADS:]
    col_scale = jnp.concatenate([jnp.full((W_GRP,), scale, F32), jnp.ones((3 * W_GRP,), F32)])
    w_main = jnp.concatenate([wa * col_scale, wb * col_scale], axis=2).astype(BF16)
    wf_pad = jnp.pad(wf, ((0, 0), (0, 0), (0, LANES - N_HEADS))).astype(BF16)
    bf_pad = jnp.pad(b_forget.astype(F32), ((0, 0), (0, LANES - N_HEADS)))[:, None, :]
    wo = w_out.astype(BF16)

    kv_view = lambda c: c.reshape(depth, n_phys, PAGE * N_HEADS, HEAD_DIM)
    cka, cva, cks, cvs = (kv_view(c) for c in (cache_fox_k, cache_fox_v, cache_sb_k, cache_sb_v))
    clf = jnp.swapaxes(cache_fox_logf, 2, 3)

    xp = x_prompt.reshape(batch * seq, d_model)
    xs = x_sample.reshape(bd * td, d_model)
    rows_p, rows_s = [], []
    heads = lambda a: a.reshape(bd, td, N_HEADS, HEAD_DIM).transpose(0, 2, 1, 3)
    for l in range(depth):
        lg, lb = ln_gain[l][None, :], ln_bias[l][None, :]
        (ka, va, ks, vs, lf, kab, ksb, qat, vat, gat, qst, vst, gst, lft) = _proj_prompt(
            xp, w_main[l], wf_pad[l], bf_pad[l])
        dt = _dcum(lft, nq)
        oat = _attn_prompt("fox", qat, kab, vat, dt, batch, nq)
        ost = _attn_prompt("sb", qst, ksb, vst, None, batch, nq)
        xp = _outproj(oat, ost, gat, gst, xp, wo[l], lg, lb, alpha, True)
        rows_p.append((ka, va, lf, ks, vs))
        (qa_s, ka_s, va_s, ga_s, qs_s, ks_s, vs_s, gs_s, lf_s) = _proj_sample(xs, w_main[l], wf_pad[l], bf_pad[l])
        lfn = lf_s.reshape(bd, td, N_HEADS)
        oa_s, os_s = _decode_attn(l, page_table, heads(qa_s), heads(qs_s), heads(ka_s), heads(va_s),
                                  heads(ks_s), heads(vs_s), lfn, lfn.transpose(0, 2, 1),
                                  cka, cva, cks, cvs, clf)
        unheads = lambda a: a.transpose(0, 2, 1, 3).reshape(bd * td, W_GRP)
        xs = _outproj(unheads(oa_s), unheads(os_s), ga_s, gs_s, xs, wo[l], lg, lb, alpha, False)
        rows_s.append((ka_s, va_s, lf_s, ks_s, vs_s))

    def stack(rows, idx, shape):
        return jnp.stack([r[idx] for r in rows]).reshape((depth,) + shape)

    kv_p, lf_p = (batch, seq, N_HEADS, HEAD_DIM), (batch, seq, N_HEADS)
    kv_s, lf_sh = (bd, td, N_HEADS, HEAD_DIM), (bd, td, N_HEADS)
    return (xp.reshape(batch, seq, d_model), xs.reshape(bd, td, d_model),
            stack(rows_p, 0, kv_p), stack(rows_p, 1, kv_p), stack(rows_p, 2, lf_p),
            stack(rows_p, 3, kv_p), stack(rows_p, 4, kv_p),
            stack(rows_s, 0, kv_s), stack(rows_s, 1, kv_s), stack(rows_s, 2, lf_sh),
            stack(rows_s, 3, kv_s), stack(rows_s, 4, kv_s))
```

```python
import functools

import jax
import jax.numpy as jnp
from jax import lax
from jax.experimental import pallas as pl
from jax.experimental.pallas import tpu as pltpu

F32 = jnp.float32
BF16 = jnp.bfloat16

HEAD_DIM = 64
N_HEADS = 8
W_GRP = N_HEADS * HEAD_DIM
PAGE = 128
LANES = 128
LN_EPS = 1e-5
NEG = -1e30
VMEM_LIMIT = 56 * 1024 * 1024
QK_SCALE = HEAD_DIM ** -0.5

T_TILE = 512
K_SUB = 256
PAGES_PER_STEP = 8


def _log_sigmoid(x):
    return jnp.minimum(x, 0.0) - jnp.log(1.0 + jnp.exp(-jnp.abs(x)))


def _softplus(x):
    return jnp.maximum(x, 0.0) + jnp.log(1.0 + jnp.exp(-jnp.abs(x)))


def _silu(g):
    return g * (1.0 / (1.0 + jnp.exp(-g)))


def _cparams(sem):
    return pltpu.CompilerParams(dimension_semantics=sem, vmem_limit_bytes=VMEM_LIMIT)


def _pick_row(x, idx):
    sub = lax.broadcasted_iota(jnp.int32, x.shape, 0)
    return jnp.sum(jnp.where(sub == idx, x, 0.0), axis=0, keepdims=True)


def _proj_prompt_kernel(x_ref, w_ref, wf_ref, bf_ref, *rest):
    (kat_ref, vat_ref, kst_ref, vst_ref, lft_ref,
     kab_ref, ksb_ref, qat_ref, gat_ref, qst_ref, gst_ref) = rest[-11:]
    xb = x_ref[...].astype(BF16)

    def chunk(c):
        return jnp.dot(xb, w_ref[:, c * W_GRP:(c + 1) * W_GRP], preferred_element_type=F32)

    qat_ref[0] = (chunk(0) * QK_SCALE).T.astype(BF16)
    ka = chunk(1)
    kab_ref[...] = ka.astype(BF16)
    kat_ref[...] = ka.T
    vat_ref[...] = chunk(2).T
    gat_ref[0] = chunk(3).T
    qst_ref[0] = (chunk(4) * QK_SCALE).T.astype(BF16)
    ks = chunk(5)
    ksb_ref[...] = ks.astype(BF16)
    kst_ref[...] = ks.T
    vst_ref[...] = chunk(6).T
    gst_ref[0] = chunk(7).T
    fa = jnp.dot(xb, wf_ref[...], preferred_element_type=F32) + bf_ref[...]
    lft_ref[...] = _log_sigmoid(fa).T[:N_HEADS, :]


def _proj_prompt(layer, x, w, wf, bfp, bufs, depth, batch, nq):
    m, d = x.shape
    tm = T_TILE
    nt = m // tm
    seq = nq * tm
    kv_buf = jax.ShapeDtypeStruct((depth, batch, W_GRP, seq), F32)
    lf_buf = jax.ShapeDtypeStruct((depth, batch, N_HEADS, seq), F32)
    buf_spec = lambda r: pl.BlockSpec((None, None, r, tm), lambda i: (layer, i // nq, 0, i % nq))
    tok = jax.ShapeDtypeStruct((m, W_GRP), BF16)
    tok_spec = pl.BlockSpec((tm, W_GRP), lambda i: (i, 0))
    til = lambda dt: jax.ShapeDtypeStruct((nt, W_GRP, tm), dt)
    til_spec = pl.BlockSpec((1, W_GRP, tm), lambda i: (i, 0, 0))
    out_shape = [kv_buf] * 4 + [lf_buf, tok, tok, til(BF16), til(F32), til(BF16), til(F32)]
    out_specs = [buf_spec(W_GRP)] * 4 + [buf_spec(N_HEADS), tok_spec, tok_spec] + [til_spec] * 4
    in_specs = [pl.BlockSpec((tm, d), lambda i: (i, 0)),
                pl.BlockSpec(w.shape, lambda i: (0, 0)),
                pl.BlockSpec(wf.shape, lambda i: (0, 0)),
                pl.BlockSpec(bfp.shape, lambda i: (0, 0))]
    args = [x, w, wf, bfp]
    aliases = {}
    if bufs is not None:
        in_specs += [pl.BlockSpec(memory_space=pl.ANY)] * 5
        args += list(bufs)
        aliases = {4 + k: k for k in range(5)}
    return pl.pallas_call(
        _proj_prompt_kernel,
        grid=(nt,),
        in_specs=in_specs,
        out_specs=out_specs,
        out_shape=out_shape,
        input_output_aliases=aliases,
        compiler_params=_cparams(("arbitrary",)),
        name="proj_prompt",
    )(*args)


def _proj_sample_kernel(x_ref, w_ref, wf_ref, bf_ref,
                        qa_ref, ka_ref, va_ref, ga_ref, qs_ref, ks_ref, vs_ref, gs_ref, lf_ref, lfp_ref):
    xb = x_ref[...].astype(BF16)
    outs = (qa_ref, ka_ref, va_ref, ga_ref, qs_ref, ks_ref, vs_ref, gs_ref)
    for c, o_ref in enumerate(outs):
        r = jnp.dot(xb, w_ref[:, c * W_GRP:(c + 1) * W_GRP], preferred_element_type=F32)
        o_ref[...] = r * QK_SCALE if c % 4 == 0 else r
    fa = jnp.dot(xb, wf_ref[...], preferred_element_type=F32) + bf_ref[...]
    lf = _log_sigmoid(fa)
    lf_ref[...] = lf[:, :N_HEADS]
    lfp_ref[...] = lf


def _proj_sample(x, w, wf, bfp):
    m, _ = x.shape
    full = lambda a: pl.BlockSpec(a.shape, lambda i: (0,) * a.ndim)
    shapes = [(m, W_GRP)] * 8 + [(m, N_HEADS), (m, LANES)]
    return pl.pallas_call(
        _proj_sample_kernel,
        grid=(1,),
        in_specs=[full(x), full(w), full(wf), full(bfp)],
        out_specs=[pl.BlockSpec(s, lambda i: (0, 0)) for s in shapes],
        out_shape=[jax.ShapeDtypeStruct(s, F32) for s in shapes],
        compiler_params=_cparams(("arbitrary",)),
        name="proj_sample",
    )(x, w, wf, bfp)


def _dcum_kernel(lft_ref, dt_ref, *, nq):
    x = lft_ref[...]
    t = x.shape[1]
    lane = lax.broadcasted_iota(jnp.int32, x.shape, 1)
    s = 1
    while s < t:
        x = x + jnp.where(lane >= s, pltpu.roll(x, s, axis=1), 0.0)
        s *= 2
    for i in range(nq):
        dt_ref[i] = x[:, i * T_TILE:(i + 1) * T_TILE]


def _dcum(layer, lf_buf, nq):
    _, batch, _, seq = lf_buf.shape
    return pl.pallas_call(
        functools.partial(_dcum_kernel, nq=nq),
        grid=(batch,),
        in_specs=[pl.BlockSpec((None, None, N_HEADS, seq), lambda b: (layer, b, 0, 0))],
        out_specs=pl.BlockSpec((nq, N_HEADS, T_TILE), lambda b: (b, 0, 0)),
        out_shape=jax.ShapeDtypeStruct((batch * nq, N_HEADS, T_TILE), F32),
        compiler_params=_cparams(("arbitrary",)),
        name="forget_cumsum",
    )(lf_buf)


def _pair_q(q):
    top = lax.broadcasted_iota(jnp.int32, q.shape, 0) < HEAD_DIM
    zero = jnp.zeros_like(q)
    return jnp.concatenate([jnp.where(top, q, zero), jnp.where(top, zero, q)], axis=1)


def _fox_prompt_kernel(q_ref, k_ref, v_ref, d_ref, o_ref, dk_ref, *, nq):
    tq, tk = T_TILE, K_SUB
    nsub, nrep, w = tq // tk, tq // LANES, 2 * tq
    pair = pl.program_id(1)
    for kt in range(nq):
        for c in range(nrep):
            tile = d_ref[kt, :, c * LANES:(c + 1) * LANES]
            for e in range(2):
                row = _pick_row(tile, 2 * pair + e)
                dk_ref[kt * tq + c * LANES:kt * tq + (c + 1) * LANES, e * LANES:(e + 1) * LANES] = (
                    jnp.broadcast_to(row, (LANES, LANES)).T)
    row_i = lax.broadcasted_iota(jnp.int32, (tk, w), 0)
    col_i = lax.broadcasted_iota(jnp.int32, (tk, w), 1) % tq

    def block(kt, jj, carry, qm, dq, mask):
        m, l, acc_e, acc_o = carry
        r0 = pl.multiple_of(kt * tq + jj * tk, tk)
        s = jnp.dot(k_ref[pl.ds(r0, tk), :], qm, preferred_element_type=F32)
        dk = dk_ref[pl.ds(r0, tk), :]
        dkw = jnp.concatenate([dk[:, :LANES]] * nrep + [dk[:, LANES:]] * nrep, axis=1)
        s = s + (dq - dkw)
        if mask is not None:
            s = jnp.where(mask, s, NEG)
        m_new = jnp.maximum(m, jnp.max(s, axis=0, keepdims=True))
        alpha = jnp.exp(m - m_new)
        p = jnp.exp(s - m_new)
        l = alpha * l + jnp.sum(p, axis=0, keepdims=True)
        pb = p.astype(BF16)
        vb = v_ref[:, pl.ds(r0, tk)].astype(BF16)
        acc_e = alpha[:, :tq] * acc_e + jnp.dot(vb[:HEAD_DIM], pb[:, :tq], preferred_element_type=F32)
        acc_o = alpha[:, tq:] * acc_o + jnp.dot(vb[HEAD_DIM:], pb[:, tq:], preferred_element_type=F32)
        return m_new, l, acc_e, acc_o

    def q_tile(qi, _):
        qm = _pair_q(q_ref[qi])
        dq = jnp.concatenate([_pick_row(d_ref[qi], 2 * pair), _pick_row(d_ref[qi], 2 * pair + 1)], axis=1)
        zacc = jnp.zeros((HEAD_DIM, tq), F32)
        carry = (jnp.full((1, w), NEG, F32), jnp.zeros((1, w), F32), zacc, zacc)

        def full_tile(kt, c):
            for jj in range(nsub):
                c = block(kt, jj, c, qm, dq, None)
            return c

        carry = lax.fori_loop(0, qi, full_tile, carry)
        for jj in range(nsub):
            carry = block(qi, jj, carry, qm, dq, (row_i + jj * tk) <= col_i)
        _, l, acc_e, acc_o = carry
        inv = 1.0 / l
        o_ref[qi] = jnp.concatenate([acc_e * inv[:, :tq], acc_o * inv[:, tq:]], axis=0)
        return 0

    lax.fori_loop(0, nq, q_tile, 0)


def _sb_prompt_kernel(q_ref, k_ref, v_ref, o_ref, *, nq):
    tq, tk = T_TILE, K_SUB
    nsub, w = tq // tk, 2 * tq
    row_i = lax.broadcasted_iota(jnp.int32, (tk, w), 0)
    col_i = lax.broadcasted_iota(jnp.int32, (tk, w), 1) % tq
    ui = lax.broadcasted_iota(jnp.int32, (tk, tk), 0)
    uj = lax.broadcasted_iota(jnp.int32, (tk, tk), 1)
    upper = jnp.where(uj >= ui, 1.0, 0.0).astype(BF16)

    def block(kt, jj, carry, qm, mask):
        tot, acc_e, acc_o = carry
        r0 = pl.multiple_of(kt * tq + jj * tk, tk)
        z = jnp.dot(k_ref[pl.ds(r0, tk), :], qm, preferred_element_type=F32)
        sp = _softplus(z)
        if mask is not None:
            sp = jnp.where(mask, sp, 0.0)
        cs = jnp.dot(upper, sp.astype(BF16), preferred_element_type=F32)
        a = jnp.exp(z - cs - tot)
        if mask is not None:
            a = jnp.where(mask, a, 0.0)
        ab = a.astype(BF16)
        vb = v_ref[:, pl.ds(r0, tk)].astype(BF16)
        acc_e = acc_e + jnp.dot(vb[:HEAD_DIM], ab[:, :tq], preferred_element_type=F32)
        acc_o = acc_o + jnp.dot(vb[HEAD_DIM:], ab[:, tq:], preferred_element_type=F32)
        return tot + cs[0:1, :], acc_e, acc_o

    def q_tile(qi, _):
        qm = _pair_q(q_ref[qi])
        zacc = jnp.zeros((HEAD_DIM, tq), F32)
        carry = (jnp.zeros((1, w), F32), zacc, zacc)
        for jj in reversed(range(nsub)):
            carry = block(qi, jj, carry, qm, (row_i + jj * tk) < col_i)

        def full_tile(it, c):
            kt = qi - 1 - it
            for jj in reversed(range(nsub)):
                c = block(kt, jj, c, qm, None)
            return c

        carry = lax.fori_loop(0, qi, full_tile, carry)
        o_ref[qi] = jnp.concatenate([carry[1], carry[2]], axis=0)
        return 0

    lax.fori_loop(0, nq, q_tile, 0)


def _attn_prompt(kind, layer, qt, kb, vbuf, dt, batch, nq):
    t = nq * T_TILE
    pair_rows = 2 * HEAD_DIM
    til = pl.BlockSpec((nq, pair_rows, T_TILE), lambda b, p: (b, p, 0))
    in_specs = [til,
                pl.BlockSpec((t, pair_rows), lambda b, p: (b, p)),
                pl.BlockSpec((None, None, pair_rows, t), lambda b, p: (layer, b, p, 0))]
    args = [qt, kb, vbuf]
    scratch = []
    if kind == "fox":
        in_specs.append(pl.BlockSpec((nq, N_HEADS, T_TILE), lambda b, p: (b, 0, 0)))
        args.append(dt)
        scratch.append(pltpu.VMEM((t, 2 * LANES), F32))
        body = functools.partial(_fox_prompt_kernel, nq=nq)
    else:
        body = functools.partial(_sb_prompt_kernel, nq=nq)
    return pl.pallas_call(
        body,
        grid=(batch, N_HEADS // 2),
        in_specs=in_specs,
        out_specs=til,
        out_shape=jax.ShapeDtypeStruct(qt.shape, F32),
        scratch_shapes=scratch,
        compiler_params=_cparams(("arbitrary", "arbitrary")),
        name=kind + "_attn_prompt",
    )(*args)


def _outproj_kernel(oa_ref, os_ref, ga_ref, gs_ref, x_ref, wo_ref, lg_ref, lb_ref, y_ref, *, alpha, transposed):
    if transposed:
        ua = (oa_ref[0] * _silu(ga_ref[0])).T.astype(BF16)
        us = (os_ref[0] * _silu(gs_ref[0])).T.astype(BF16)
    else:
        ua = (oa_ref[...] * _silu(ga_ref[...])).astype(BF16)
        us = (os_ref[...] * _silu(gs_ref[...])).astype(BF16)
    y = (jnp.dot(ua, wo_ref[:W_GRP, :], preferred_element_type=F32)
         + jnp.dot(us, wo_ref[W_GRP:, :], preferred_element_type=F32))
    r = alpha * x_ref[...] + y
    mu = jnp.mean(r, axis=-1, keepdims=True)
    c = r - mu
    var = jnp.mean(c * c, axis=-1, keepdims=True)
    y_ref[...] = c * lax.rsqrt(var + LN_EPS) * lg_ref[...] + lb_ref[...]


def _outproj(oa, os_, ga, gs, x, wo, lg, lb, alpha, transposed):
    m, d = x.shape
    if transposed:
        tm = T_TILE
        o_spec = pl.BlockSpec((1, W_GRP, tm), lambda i: (i, 0, 0))
    else:
        tm = m
        o_spec = pl.BlockSpec((tm, W_GRP), lambda i: (i, 0))
    vec = pl.BlockSpec((1, d), lambda i: (0, 0))
    return pl.pallas_call(
        functools.partial(_outproj_kernel, alpha=alpha, transposed=transposed),
        grid=(m // tm,),
        in_specs=[o_spec] * 4 + [pl.BlockSpec((tm, d), lambda i: (i, 0)),
                                 pl.BlockSpec(wo.shape, lambda i: (0, 0)), vec, vec],
        out_specs=pl.BlockSpec((tm, d), lambda i: (i, 0)),
        out_shape=jax.ShapeDtypeStruct((m, d), F32),
        compiler_params=_cparams(("arbitrary",)),
        name="outproj_ln_" + ("prompt" if transposed else "sample"),
    )(oa, os_, ga, gs, x, wo, lg, lb)


def _decode_kernel(pt_ref, qa_ref, qs_ref, kna_ref, vna_ref, kns_ref, vns_ref, lfp_ref, *rest, pg, td):
    del pt_ref
    ka_refs, va_refs = rest[0:pg], rest[pg:2 * pg]
    ks_refs, vs_refs = rest[2 * pg:3 * pg], rest[3 * pg:4 * pg]
    lf_refs = rest[4 * pg:5 * pg]
    oa_ref, os_ref = rest[5 * pg], rest[5 * pg + 1]
    qda_ref, qds_ref, m_ref, l_ref, acca_ref, rc_ref, g_ref, tot_ref, accs_ref = rest[5 * pg + 2:]
    rows = N_HEADS * td
    j = pl.program_id(1)
    lane = lax.broadcasted_iota(jnp.int32, (rows, PAGE), 1)
    tok = lax.broadcasted_iota(jnp.int32, (rows, PAGE), 0) % td
    lane_h = lax.broadcasted_iota(jnp.int32, (N_HEADS, PAGE), 1)
    ui = lax.broadcasted_iota(jnp.int32, (PAGE, PAGE), 0)
    uj = lax.broadcasted_iota(jnp.int32, (PAGE, PAGE), 1)
    later = jnp.where(ui >= uj, 1.0, 0.0).astype(BF16)
    own_cols = (lax.broadcasted_iota(jnp.int32, (rows, W_GRP), 1) // HEAD_DIM
                == lax.broadcasted_iota(jnp.int32, (rows, W_GRP), 0) // td)

    def rep_heads(r):
        return jnp.concatenate(
            [jnp.broadcast_to(r[h:h + 1, :], (td, r.shape[1])) for h in range(N_HEADS)], axis=0)

    def tile_lanes(x, n):
        return x if n == 1 else jnp.concatenate([x] * n, axis=1)

    def nt_dot(p, vt):
        return lax.dot_general(p.astype(BF16), vt, (((1,), (1,)), ((), ())), preferred_element_type=F32)

    def fox_block(kt, vt, r_keys, mask, n):
        s = jnp.dot(qda_ref[...], kt, preferred_element_type=F32)
        s = s + tile_lanes(g_ref[...], n) + rep_heads(r_keys)
        if mask is not None:
            s = jnp.where(mask, s, NEG)
        m_prev = m_ref[...]
        m_new = jnp.maximum(m_prev, jnp.max(s, axis=1, keepdims=True))
        alpha = jnp.exp(m_prev - m_new)
        p = jnp.exp(s - tile_lanes(m_new, n))
        l_ref[...] = alpha * l_ref[...] + jnp.sum(p, axis=1, keepdims=True)
        m_ref[...] = m_new
        acca_ref[...] = tile_lanes(alpha, W_GRP // PAGE) * acca_ref[...] + nt_dot(p, vt)

    def sb_block(kt, vt, mask, n):
        z = jnp.dot(qds_ref[...], kt, preferred_element_type=F32)
        sp = _softplus(z)
        if mask is not None:
            sp = jnp.where(mask, sp, 0.0)
        stack = sp if n == 1 else jnp.concatenate([sp[:, i * PAGE:(i + 1) * PAGE] for i in range(n)], axis=0)
        cs = jnp.dot(stack.astype(BF16), later, preferred_element_type=F32)
        carry = tot_ref[...]
        parts = [None] * n
        for i in reversed(range(n)):
            cs_i = cs[i * rows:(i + 1) * rows, :]
            a_i = jnp.exp(z[:, i * PAGE:(i + 1) * PAGE] - cs_i - carry)
            parts[i] = a_i
            carry = carry + jnp.broadcast_to(cs_i[:, 0:1], (rows, PAGE))
        tot_ref[...] = carry
        a = parts[0] if n == 1 else jnp.concatenate(parts, axis=1)
        if mask is not None:
            a = jnp.where(mask, a, 0.0)
        accs_ref[...] = accs_ref[...] + nt_dot(a, vt)

    @pl.when(j == 0)
    def _():
        m_ref[...] = jnp.full(m_ref.shape, NEG, F32)
        l_ref[...] = jnp.zeros(l_ref.shape, F32)
        acca_ref[...] = jnp.zeros(acca_ref.shape, F32)
        rc_ref[...] = jnp.zeros(rc_ref.shape, F32)
        tot_ref[...] = jnp.zeros(tot_ref.shape, F32)
        accs_ref[...] = jnp.zeros(accs_ref.shape, F32)
        for q_ref, qd_ref in ((qa_ref, qda_ref), (qs_ref, qds_ref)):
            q8 = jnp.concatenate([q_ref[0]] * N_HEADS, axis=0)
            qd_ref[...] = jnp.where(own_cols, q8, 0.0).astype(BF16)
        pad_rows = lambda x: jnp.concatenate([x, jnp.zeros((PAGE - td, x.shape[1]), F32)], axis=0)
        lfnt = pad_rows(lfp_ref[0]).T[:N_HEADS, :]
        gt = lfnt
        s = 1
        while s < td:
            gt = gt + jnp.where(lane_h >= s, pltpu.roll(gt, s, axis=1), 0.0)
            s *= 2
        g_ref[...] = jnp.broadcast_to(
            jnp.sum(jnp.where(lane == tok, rep_heads(gt), 0.0), axis=1, keepdims=True), (rows, PAGE))
        new_t = lambda ref: pad_rows(ref[0]).T.astype(BF16)
        fox_block(new_t(kna_ref), new_t(vna_ref), -gt, lane <= tok, 1)
        sb_block(new_t(kns_ref), new_t(vns_ref), lane < tok, 1)

    def pages_t(refs):
        return jnp.concatenate([r[...].reshape(W_GRP, PAGE) for r in refs], axis=1).astype(BF16)

    r_parts = [None] * pg
    rc = rc_ref[...]
    for i in reversed(range(pg)):
        lf = lf_refs[i][...]
        x = lf
        s = 1
        while s < PAGE:
            x = x + jnp.where(lane_h + s < PAGE, pltpu.roll(x, PAGE - s, axis=1), 0.0)
            s *= 2
        r_parts[i] = (x - lf) + rc
        rc = rc + jnp.broadcast_to(x[:, 0:1], (N_HEADS, PAGE))
    rc_ref[...] = rc
    fox_block(pages_t(ka_refs), pages_t(va_refs), jnp.concatenate(r_parts, axis=1), None, pg)
    sb_block(pages_t(ks_refs), pages_t(vs_refs), None, pg)

    @pl.when(j == pl.num_programs(1) - 1)
    def _():
        inv = tile_lanes(1.0 / l_ref[...], W_GRP // PAGE)
        for acc, o_ref in ((acca_ref[...] * inv, oa_ref), (accs_ref[...], os_ref)):
            acc = jnp.where(own_cols, acc, 0.0)
            out = acc[0:td]
            for h in range(1, N_HEADS):
                out = out + acc[h * td:(h + 1) * td]
            o_ref[0] = out


def _decode_attn(layer, page_table, qa, qs, kna, vna, kns, vns, lfp, cka, cva, cks, cvs, clf):
    bd, n_pages = page_table.shape
    td = qa.shape[1]
    pg = PAGES_PER_STEP
    rows = N_HEADS * td
    small = lambda a: pl.BlockSpec((1,) + a.shape[1:], lambda b, j, pt: (b,) + (0,) * (a.ndim - 1))

    def page_spec(i, shape):
        def imap(b, j, pt):
            return (layer, pt[b, n_pages - (j + 1) * pg + i]) + (0,) * len(shape)
        return pl.BlockSpec((None, None) + shape, imap)

    in_specs = [small(a) for a in (qa, qs, kna, vna, kns, vns, lfp)]
    args = [qa, qs, kna, vna, kns, vns, lfp]
    for cache in (cka, cva, cks, cvs):
        in_specs += [page_spec(i, (N_HEADS, HEAD_DIM, PAGE)) for i in range(pg)]
        args += [cache] * pg
    in_specs += [page_spec(i, (N_HEADS, PAGE)) for i in range(pg)]
    args += [clf] * pg
    out_spec = pl.BlockSpec((1, td, W_GRP), lambda b, j, pt: (b, 0, 0))
    out_shape = jax.ShapeDtypeStruct((bd, td, W_GRP), F32)
    stat = pltpu.VMEM((rows, PAGE), F32)
    accum = pltpu.VMEM((rows, W_GRP), F32)
    qdiag = pltpu.VMEM((rows, W_GRP), BF16)
    grid_spec = pltpu.PrefetchScalarGridSpec(
        num_scalar_prefetch=1,
        grid=(bd, n_pages // pg),
        in_specs=in_specs,
        out_specs=[out_spec, out_spec],
        scratch_shapes=[qdiag, qdiag, stat, stat, accum, pltpu.VMEM((N_HEADS, PAGE), F32), stat, stat, accum],
    )
    return pl.pallas_call(
        functools.partial(_decode_kernel, pg=pg, td=td),
        grid_spec=grid_spec,
        out_shape=[out_shape, out_shape],
        compiler_params=_cparams(("arbitrary", "arbitrary")),
        name="decode_attn",
    )(page_table, *args)


def kernel(x_prompt, x_sample, cache_fox_k, cache_fox_v, cache_fox_logf, cache_sb_k, cache_sb_v,
           page_table, w_in, b_forget, w_out, ln_gain, ln_bias):
    batch, seq, d_model = x_prompt.shape
    bd, td, _ = x_sample.shape
    depth = w_in.shape[0]
    assert seq % T_TILE == 0 and page_table.shape[1] % PAGES_PER_STEP == 0
    assert cache_fox_k.shape[2:] == (PAGE, N_HEADS, HEAD_DIM) and w_in.shape[2] == 8 * W_GRP + N_HEADS
    nq = seq // T_TILE
    alpha = (2 * depth) ** 0.25

    ga_end = 4 * W_GRP
    w_main = jnp.concatenate([w_in[:, :, :ga_end], w_in[:, :, ga_end + N_HEADS:]], axis=2).astype(BF16)
    wf_pad = jnp.pad(w_in[:, :, ga_end:ga_end + N_HEADS], ((0, 0), (0, 0), (0, LANES - N_HEADS))).astype(BF16)
    bf_pad = jnp.pad(b_forget.astype(F32), ((0, 0), (0, LANES - N_HEADS)))[:, None, :]
    wo = w_out.astype(BF16)

    kv_view = lambda c: c.transpose(0, 1, 3, 4, 2)
    cka, cva, cks, cvs = (kv_view(c) for c in (cache_fox_k, cache_fox_v, cache_sb_k, cache_sb_v))
    clf = jnp.swapaxes(cache_fox_logf, 2, 3)

    xp = x_prompt.reshape(batch * seq, d_model)
    xs = x_sample.reshape(bd * td, d_model)
    bufs = None
    rows_s = []
    per_b = lambda a: a.reshape(bd, td, a.shape[-1])
    for l in range(depth):
        lg, lb = ln_gain[l][None, :], ln_bias[l][None, :]
        outs = _proj_prompt(l, xp, w_main[l], wf_pad[l], bf_pad[l], bufs, depth, batch, nq)
        bufs = outs[:5]
        kab, ksb, qat, gat, qst, gst = outs[5:]
        dt = _dcum(l, bufs[4], nq)
        oat = _attn_prompt("fox", l, qat, kab, bufs[1], dt, batch, nq)
        ost = _attn_prompt("sb", l, qst, ksb, bufs[3], None, batch, nq)
        xp = _outproj(oat, ost, gat, gst, xp, wo[l], lg, lb, alpha, True)
        (qa_s, ka_s, va_s, ga_s, qs_s, ks_s, vs_s, gs_s, lf_s, lfp_s) = _proj_sample(
            xs, w_main[l], wf_pad[l], bf_pad[l])
        oa_s, os_s = _decode_attn(l, page_table, per_b(qa_s), per_b(qs_s), per_b(ka_s), per_b(va_s),
                                  per_b(ks_s), per_b(vs_s), per_b(lfp_s), cka, cva, cks, cvs, clf)
        xs = _outproj(oa_s.reshape(bd * td, W_GRP), os_s.reshape(bd * td, W_GRP), ga_s, gs_s, xs,
                      wo[l], lg, lb, alpha, False)
        rows_s.append((ka_s, va_s, lf_s, ks_s, vs_s))

    kv_out = lambda b: b.reshape(depth, batch, N_HEADS, HEAD_DIM, seq).transpose(0, 1, 4, 2, 3)
    lf_out = bufs[4].transpose(0, 1, 3, 2)

    def stack_s(idx, shape):
        return jnp.stack([r[idx] for r in rows_s]).reshape((depth,) + shape)

    kv_s, lf_sh = (bd, td, N_HEADS, HEAD_DIM), (bd, td, N_HEADS)
    return (xp.reshape(batch, seq, d_model), xs.reshape(bd, td, d_model),
            kv_out(bufs[0]), kv_out(bufs[1]), lf_out, kv_out(bufs[2]), kv_out(bufs[3]),
            stack_s(0, kv_s), stack_s(1, kv_s), stack_s(2, lf_sh), stack_s(3, kv_s), stack_s(4, kv_s))
```

```python
import functools

import jax
import jax.numpy as jnp
from jax import lax
from jax.experimental import pallas as pl
from jax.experimental.pallas import tpu as pltpu

F32 = jnp.float32
BF16 = jnp.bfloat16

HEAD_DIM = 64
N_HEADS = 8
W_GRP = N_HEADS * HEAD_DIM
PAGE = 128
LANES = 128
LN_EPS = 1e-5
NEG = -1e30
VMEM_LIMIT = 56 * 1024 * 1024
QK_SCALE = HEAD_DIM ** -0.5

T_TILE = 512
K_SUB = 256
SB_PAIRS = 2
PAGES_PER_STEP = 8


def _log_sigmoid(x):
    return jnp.minimum(x, 0.0) - jnp.log(1.0 + jnp.exp(-jnp.abs(x)))


def _neg_abs(x):
    bits = lax.bitcast_convert_type(x, jnp.uint32) | jnp.uint32(0x80000000)
    return lax.bitcast_convert_type(bits, F32)


def _softplus(x):
    return jnp.maximum(x, 0.0) + jnp.log(1.0 + jnp.exp(_neg_abs(x)))


def _silu(g):
    return g * (1.0 / (1.0 + jnp.exp(-g)))


def _cparams(sem):
    return pltpu.CompilerParams(dimension_semantics=sem, vmem_limit_bytes=VMEM_LIMIT)


def _pick_row(x, idx):
    sub = lax.broadcasted_iota(jnp.int32, x.shape, 0)
    return jnp.sum(jnp.where(sub == idx, x, 0.0), axis=0, keepdims=True)


def _proj_prompt_kernel(x_ref, w_ref, wf_ref, bf_ref, *rest):
    (kat_ref, vat_ref, kst_ref, vst_ref, lft_ref,
     kab_ref, ksb_ref, qat_ref, gat_ref, qst_ref, gst_ref) = rest[-11:]
    xb = x_ref[...].astype(BF16)

    def chunk(c):
        return jnp.dot(xb, w_ref[:, c * W_GRP:(c + 1) * W_GRP], preferred_element_type=F32)

    qat_ref[0] = (chunk(0) * QK_SCALE).T.astype(BF16)
    ka = chunk(1)
    kab_ref[...] = ka.astype(BF16)
    kat_ref[...] = ka.T
    vat_ref[...] = chunk(2).T
    gat_ref[0] = chunk(3).T.astype(BF16)
    qst_ref[0] = (chunk(4) * QK_SCALE).T.astype(BF16)
    ks = chunk(5)
    ksb_ref[...] = ks.astype(BF16)
    kst_ref[...] = ks.T
    vst_ref[...] = chunk(6).T
    gst_ref[0] = chunk(7).T.astype(BF16)
    fa = jnp.dot(xb, wf_ref[...], preferred_element_type=F32) + bf_ref[...]
    lft_ref[...] = _log_sigmoid(fa).T[:N_HEADS, :]


def _proj_prompt(layer, x, w, wf, bfp, bufs, depth, batch, nq):
    m, d = x.shape
    tm = T_TILE
    nt = m // tm
    seq = nq * tm
    kv_buf = jax.ShapeDtypeStruct((depth, batch, W_GRP, seq), F32)
    lf_buf = jax.ShapeDtypeStruct((depth, batch, N_HEADS, seq), F32)
    buf_spec = lambda r: pl.BlockSpec((None, None, r, tm), lambda i: (layer, i // nq, 0, i % nq))
    tok = jax.ShapeDtypeStruct((m, W_GRP), BF16)
    tok_spec = pl.BlockSpec((tm, W_GRP), lambda i: (i, 0))
    til = jax.ShapeDtypeStruct((nt, W_GRP, tm), BF16)
    til_spec = pl.BlockSpec((1, W_GRP, tm), lambda i: (i, 0, 0))
    out_shape = [kv_buf] * 4 + [lf_buf, tok, tok] + [til] * 4
    out_specs = [buf_spec(W_GRP)] * 4 + [buf_spec(N_HEADS), tok_spec, tok_spec] + [til_spec] * 4
    in_specs = [pl.BlockSpec((tm, d), lambda i: (i, 0)),
                pl.BlockSpec(w.shape, lambda i: (0, 0)),
                pl.BlockSpec(wf.shape, lambda i: (0, 0)),
                pl.BlockSpec(bfp.shape, lambda i: (0, 0))]
    args = [x, w, wf, bfp]
    aliases = {}
    if bufs is not None:
        in_specs += [pl.BlockSpec(memory_space=pl.ANY)] * 5
        args += list(bufs)
        aliases = {4 + k: k for k in range(5)}
    return pl.pallas_call(
        _proj_prompt_kernel,
        grid=(nt,),
        in_specs=in_specs,
        out_specs=out_specs,
        out_shape=out_shape,
        input_output_aliases=aliases,
        compiler_params=_cparams(("arbitrary",)),
        name="proj_prompt",
    )(*args)


def _proj_sample_kernel(x_ref, w_ref, wf_ref, bf_ref,
                        qa_ref, ka_ref, va_ref, ga_ref, qs_ref, ks_ref, vs_ref, gs_ref, lf_ref, lfp_ref):
    xb = x_ref[...].astype(BF16)
    outs = (qa_ref, ka_ref, va_ref, ga_ref, qs_ref, ks_ref, vs_ref, gs_ref)
    for c, o_ref in enumerate(outs):
        r = jnp.dot(xb, w_ref[:, c * W_GRP:(c + 1) * W_GRP], preferred_element_type=F32)
        o_ref[...] = r * QK_SCALE if c % 4 == 0 else r
    fa = jnp.dot(xb, wf_ref[...], preferred_element_type=F32) + bf_ref[...]
    lf = _log_sigmoid(fa)
    lf_ref[...] = lf[:, :N_HEADS]
    lfp_ref[...] = lf


def _proj_sample(x, w, wf, bfp):
    m, _ = x.shape
    full = lambda a: pl.BlockSpec(a.shape, lambda i: (0,) * a.ndim)
    shapes = [(m, W_GRP)] * 8 + [(m, N_HEADS), (m, LANES)]
    return pl.pallas_call(
        _proj_sample_kernel,
        grid=(1,),
        in_specs=[full(x), full(w), full(wf), full(bfp)],
        out_specs=[pl.BlockSpec(s, lambda i: (0, 0)) for s in shapes],
        out_shape=[jax.ShapeDtypeStruct(s, F32) for s in shapes],
        compiler_params=_cparams(("arbitrary",)),
        name="proj_sample",
    )(x, w, wf, bfp)


def _dcum_kernel(lft_ref, dt_ref, *, nq):
    x = lft_ref[...]
    t = x.shape[1]
    lane = lax.broadcasted_iota(jnp.int32, x.shape, 1)
    s = 1
    while s < t:
        x = x + jnp.where(lane >= s, pltpu.roll(x, s, axis=1), 0.0)
        s *= 2
    for i in range(nq):
        dt_ref[i] = x[:, i * T_TILE:(i + 1) * T_TILE]


def _dcum(layer, lf_buf, nq):
    _, batch, _, seq = lf_buf.shape
    return pl.pallas_call(
        functools.partial(_dcum_kernel, nq=nq),
        grid=(batch,),
        in_specs=[pl.BlockSpec((None, None, N_HEADS, seq), lambda b: (layer, b, 0, 0))],
        out_specs=pl.BlockSpec((nq, N_HEADS, T_TILE), lambda b: (b, 0, 0)),
        out_shape=jax.ShapeDtypeStruct((batch * nq, N_HEADS, T_TILE), F32),
        compiler_params=_cparams(("arbitrary",)),
        name="forget_cumsum",
    )(lf_buf)


def _pair_q(q):
    top = lax.broadcasted_iota(jnp.int32, q.shape, 0) < HEAD_DIM
    zero = jnp.zeros_like(q)
    return jnp.concatenate([jnp.where(top, q, zero), jnp.where(top, zero, q)], axis=1)


def _fox_prompt_kernel(q_ref, k_ref, v_ref, d_ref, o_ref, dk_ref, *, nq):
    tq, tk = T_TILE, K_SUB
    nsub, nrep, w = tq // tk, tq // LANES, 2 * tq
    pair = pl.program_id(1)
    for kt in range(nq):
        for c in range(nrep):
            tile = d_ref[kt, :, c * LANES:(c + 1) * LANES]
            for e in range(2):
                row = _pick_row(tile, 2 * pair + e)
                dk_ref[kt * tq + c * LANES:kt * tq + (c + 1) * LANES, e * LANES:(e + 1) * LANES] = (
                    jnp.broadcast_to(row, (LANES, LANES)).T)
    row_i = lax.broadcasted_iota(jnp.int32, (tk, w), 0)
    col_i = lax.broadcasted_iota(jnp.int32, (tk, w), 1) % tq

    def tile(kt, carry, qm, dq, masks):
        m, l, acc_e, acc_o = carry
        r0s = [pl.multiple_of(kt * tq + jj * tk, tk) for jj in range(nsub)]
        ss = []
        for jj, r0 in enumerate(r0s):
            s = jnp.dot(k_ref[pl.ds(r0, tk), :], qm, preferred_element_type=F32)
            dk = dk_ref[pl.ds(r0, tk), :]
            s = s + (dq - jnp.concatenate([dk[:, :LANES]] * nrep + [dk[:, LANES:]] * nrep, axis=1))
            if masks is not None:
                s = jnp.where(masks[jj], s, NEG)
            ss.append(s)
        m_new = m
        for s in ss:
            m_new = jnp.maximum(m_new, jnp.max(s, axis=0, keepdims=True))
        alpha = jnp.exp(m - m_new)
        l = alpha * l
        acc_e = alpha[:, :tq] * acc_e
        acc_o = alpha[:, tq:] * acc_o
        for s, r0 in zip(ss, r0s):
            p = jnp.exp(s - m_new)
            l = l + jnp.sum(p, axis=0, keepdims=True)
            pb = p.astype(BF16)
            vb = v_ref[:, pl.ds(r0, tk)].astype(BF16)
            acc_e = acc_e + jnp.dot(vb[:HEAD_DIM], pb[:, :tq], preferred_element_type=F32)
            acc_o = acc_o + jnp.dot(vb[HEAD_DIM:], pb[:, tq:], preferred_element_type=F32)
        return m_new, l, acc_e, acc_o

    def q_tile(qi, _):
        qm = _pair_q(q_ref[qi])
        dq = jnp.concatenate([_pick_row(d_ref[qi], 2 * pair), _pick_row(d_ref[qi], 2 * pair + 1)], axis=1)
        zacc = jnp.zeros((HEAD_DIM, tq), F32)
        carry = (jnp.full((1, w), NEG, F32), jnp.zeros((1, w), F32), zacc, zacc)
        carry = lax.fori_loop(0, qi, lambda kt, c: tile(kt, c, qm, dq, None), carry)
        carry = tile(qi, carry, qm, dq, [(row_i + jj * tk) <= col_i for jj in range(nsub)])
        _, l, acc_e, acc_o = carry
        inv = 1.0 / l
        o_ref[qi] = jnp.concatenate([acc_e * inv[:, :tq], acc_o * inv[:, tq:]], axis=0).astype(BF16)
        return 0

    lax.fori_loop(0, nq, q_tile, 0)


def _sb_prompt_kernel(q_ref, k_ref, v_ref, o_ref, *, nq):
    tq, tk = T_TILE, K_SUB
    nsub, w = tq // tk, 2 * tq
    row_i = lax.broadcasted_iota(jnp.int32, (tk, w), 0)
    col_i = lax.broadcasted_iota(jnp.int32, (tk, w), 1) % tq
    ui = lax.broadcasted_iota(jnp.int32, (tk, tk), 0)
    uj = lax.broadcasted_iota(jnp.int32, (tk, tk), 1)
    upper = jnp.where(uj >= ui, 1.0, 0.0).astype(BF16)

    pr = 2 * HEAD_DIM
    npair = q_ref.shape[1] // pr

    def tile(kt, carry, qms, masks):
        r0s = [pl.multiple_of(kt * tq + jj * tk, tk) for jj in reversed(range(nsub))]
        units = [(p, i) for p in range(npair) for i in range(nsub)]
        zs = [jnp.dot(k_ref[pl.ds(r0s[i], tk), p * pr:(p + 1) * pr], qms[p], preferred_element_type=F32)
              for p, i in units]
        sps = [_softplus(z) for z in zs]
        if masks is not None:
            sps = [jnp.where(masks[i], sp, 0.0) for (p, i), sp in zip(units, sps)]
        css = [jnp.dot(upper, sp.astype(BF16), preferred_element_type=F32) for sp in sps]
        as_ = [jnp.exp(z - cs) for z, cs in zip(zs, css)]
        if masks is not None:
            as_ = [jnp.where(masks[i], a, 0.0) for (p, i), a in zip(units, as_)]
        tots, accs = list(carry[0]), list(carry[1])
        for (p, i), a, cs in zip(units, as_, css):
            ab = a.astype(BF16)
            later = jnp.exp(-tots[p])
            vb = v_ref[p * pr:(p + 1) * pr, pl.ds(r0s[i], tk)].astype(BF16)
            accs[2 * p] = accs[2 * p] + later[:, :tq] * jnp.dot(
                vb[:HEAD_DIM], ab[:, :tq], preferred_element_type=F32)
            accs[2 * p + 1] = accs[2 * p + 1] + later[:, tq:] * jnp.dot(
                vb[HEAD_DIM:], ab[:, tq:], preferred_element_type=F32)
            tots[p] = tots[p] + cs[0:1, :]
        return tuple(tots), tuple(accs)

    def q_tile(qi, _):
        q_all = q_ref[qi]
        qms = [_pair_q(q_all[p * pr:(p + 1) * pr]) for p in range(npair)]
        carry = (tuple(jnp.zeros((1, w), F32) for _ in range(npair)),
                 tuple(jnp.zeros((HEAD_DIM, tq), F32) for _ in range(2 * npair)))
        carry = tile(qi, carry, qms, [(row_i + jj * tk) < col_i for jj in reversed(range(nsub))])
        carry = lax.fori_loop(0, qi, lambda it, c: tile(qi - 1 - it, c, qms, None), carry)
        o_ref[qi] = jnp.concatenate(carry[1], axis=0).astype(BF16)
        return 0

    lax.fori_loop(0, nq, q_tile, 0)


def _attn_prompt(kind, layer, qt, kb, vbuf, dt, batch, nq):
    t = nq * T_TILE
    pair_rows = 2 * HEAD_DIM * (1 if kind == "fox" else SB_PAIRS)
    til = pl.BlockSpec((nq, pair_rows, T_TILE), lambda b, p: (b, p, 0))
    in_specs = [til,
                pl.BlockSpec((t, pair_rows), lambda b, p: (b, p)),
                pl.BlockSpec((None, None, pair_rows, t), lambda b, p: (layer, b, p, 0))]
    args = [qt, kb, vbuf]
    scratch = []
    if kind == "fox":
        in_specs.append(pl.BlockSpec((nq, N_HEADS, T_TILE), lambda b, p: (b, 0, 0)))
        args.append(dt)
        scratch.append(pltpu.VMEM((t, 2 * LANES), F32))
        body = functools.partial(_fox_prompt_kernel, nq=nq)
    else:
        body = functools.partial(_sb_prompt_kernel, nq=nq)
    return pl.pallas_call(
        body,
        grid=(batch, W_GRP // pair_rows),
        in_specs=in_specs,
        out_specs=til,
        out_shape=jax.ShapeDtypeStruct(qt.shape, BF16),
        scratch_shapes=scratch,
        compiler_params=_cparams(("arbitrary", "arbitrary")),
        name=kind + "_attn_prompt",
    )(*args)


def _outproj_kernel(oa_ref, os_ref, ga_ref, gs_ref, x_ref, wo_ref, lg_ref, lb_ref, y_ref, *, alpha, transposed):
    if transposed:
        ua = (oa_ref[0].astype(F32) * _silu(ga_ref[0].astype(F32))).T.astype(BF16)
        us = (os_ref[0].astype(F32) * _silu(gs_ref[0].astype(F32))).T.astype(BF16)
    else:
        ua = (oa_ref[...] * _silu(ga_ref[...])).astype(BF16)
        us = (os_ref[...] * _silu(gs_ref[...])).astype(BF16)
    y = (jnp.dot(ua, wo_ref[:W_GRP, :], preferred_element_type=F32)
         + jnp.dot(us, wo_ref[W_GRP:, :], preferred_element_type=F32))
    r = alpha * x_ref[...] + y
    mu = jnp.mean(r, axis=-1, keepdims=True)
    c = r - mu
    var = jnp.mean(c * c, axis=-1, keepdims=True)
    y_ref[...] = c * lax.rsqrt(var + LN_EPS) * lg_ref[...] + lb_ref[...]


def _outproj(oa, os_, ga, gs, x, wo, lg, lb, alpha, transposed):
    m, d = x.shape
    if transposed:
        tm = T_TILE
        o_spec = pl.BlockSpec((1, W_GRP, tm), lambda i: (i, 0, 0))
    else:
        tm = m
        o_spec = pl.BlockSpec((tm, W_GRP), lambda i: (i, 0))
    vec = pl.BlockSpec((1, d), lambda i: (0, 0))
    return pl.pallas_call(
        functools.partial(_outproj_kernel, alpha=alpha, transposed=transposed),
        grid=(m // tm,),
        in_specs=[o_spec] * 4 + [pl.BlockSpec((tm, d), lambda i: (i, 0)),
                                 pl.BlockSpec(wo.shape, lambda i: (0, 0)), vec, vec],
        out_specs=pl.BlockSpec((tm, d), lambda i: (i, 0)),
        out_shape=jax.ShapeDtypeStruct((m, d), F32),
        compiler_params=_cparams(("arbitrary",)),
        name="outproj_ln_" + ("prompt" if transposed else "sample"),
    )(oa, os_, ga, gs, x, wo, lg, lb)


def _decode_kernel(pt_ref, qa_ref, qs_ref, kna_ref, vna_ref, kns_ref, vns_ref, lfp_ref, *rest, pg, td):
    del pt_ref
    ka_refs, va_refs = rest[0:pg], rest[pg:2 * pg]
    ks_refs, vs_refs = rest[2 * pg:3 * pg], rest[3 * pg:4 * pg]
    lf_refs = rest[4 * pg:5 * pg]
    oa_ref, os_ref = rest[5 * pg], rest[5 * pg + 1]
    qda_ref, qds_ref, m_ref, l_ref, acca_ref, rc_ref, g_ref, tot_ref, accs_ref = rest[5 * pg + 2:]
    rows = N_HEADS * td
    j = pl.program_id(1)
    lane = lax.broadcasted_iota(jnp.int32, (rows, PAGE), 1)
    tok = lax.broadcasted_iota(jnp.int32, (rows, PAGE), 0) % td
    lane_h = lax.broadcasted_iota(jnp.int32, (N_HEADS, PAGE), 1)
    ui = lax.broadcasted_iota(jnp.int32, (PAGE, PAGE), 0)
    uj = lax.broadcasted_iota(jnp.int32, (PAGE, PAGE), 1)
    later = jnp.where(ui >= uj, 1.0, 0.0).astype(BF16)
    own_cols = (lax.broadcasted_iota(jnp.int32, (rows, W_GRP), 1) // HEAD_DIM
                == lax.broadcasted_iota(jnp.int32, (rows, W_GRP), 0) // td)

    def rep_heads(r):
        return jnp.concatenate(
            [jnp.broadcast_to(r[h:h + 1, :], (td, r.shape[1])) for h in range(N_HEADS)], axis=0)

    def tile_lanes(x, n):
        return x if n == 1 else jnp.concatenate([x] * n, axis=1)

    def nt_dot(p, vt):
        return lax.dot_general(p.astype(BF16), vt, (((1,), (1,)), ((), ())), preferred_element_type=F32)

    def attend(kt_a, vt_a, r_keys, kt_s, vt_s, mask_a, mask_s, n):
        s = jnp.dot(qda_ref[...], kt_a(), preferred_element_type=F32)
        z = jnp.dot(qds_ref[...], kt_s(), preferred_element_type=F32)
        s = s + tile_lanes(g_ref[...], n) + rep_heads(r_keys)
        if mask_a is not None:
            s = jnp.where(mask_a, s, NEG)
        sp = _softplus(z)
        if mask_s is not None:
            sp = jnp.where(mask_s, sp, 0.0)
        stack = sp if n == 1 else jnp.concatenate([sp[:, i * PAGE:(i + 1) * PAGE] for i in range(n)], axis=0)
        cs = jnp.dot(stack.astype(BF16), later, preferred_element_type=F32)
        m_prev = m_ref[...]
        m_new = jnp.maximum(m_prev, jnp.max(s, axis=1, keepdims=True))
        alpha = jnp.exp(m_prev - m_new)
        p = jnp.exp(s - tile_lanes(m_new, n))
        l_ref[...] = alpha * l_ref[...] + jnp.sum(p, axis=1, keepdims=True)
        m_ref[...] = m_new
        carry = tot_ref[...]
        parts = [None] * n
        for i in reversed(range(n)):
            cs_i = cs[i * rows:(i + 1) * rows, :]
            parts[i] = jnp.exp(z[:, i * PAGE:(i + 1) * PAGE] - cs_i - carry)
            carry = carry + jnp.broadcast_to(cs_i[:, 0:1], (rows, PAGE))
        tot_ref[...] = carry
        a = parts[0] if n == 1 else jnp.concatenate(parts, axis=1)
        if mask_s is not None:
            a = jnp.where(mask_s, a, 0.0)
        acca_ref[...] = tile_lanes(alpha, W_GRP // PAGE) * acca_ref[...] + nt_dot(p, vt_a())
        accs_ref[...] = accs_ref[...] + nt_dot(a, vt_s())

    @pl.when(j == 0)
    def _():
        m_ref[...] = jnp.full(m_ref.shape, NEG, F32)
        l_ref[...] = jnp.zeros(l_ref.shape, F32)
        acca_ref[...] = jnp.zeros(acca_ref.shape, F32)
        rc_ref[...] = jnp.zeros(rc_ref.shape, F32)
        tot_ref[...] = jnp.zeros(tot_ref.shape, F32)
        accs_ref[...] = jnp.zeros(accs_ref.shape, F32)
        for q_ref, qd_ref in ((qa_ref, qda_ref), (qs_ref, qds_ref)):
            q8 = jnp.concatenate([q_ref[0]] * N_HEADS, axis=0)
            qd_ref[...] = jnp.where(own_cols, q8, 0.0).astype(BF16)
        pad_rows = lambda x: jnp.concatenate([x, jnp.zeros((PAGE - td, x.shape[1]), F32)], axis=0)
        lfnt = pad_rows(lfp_ref[0]).T[:N_HEADS, :]
        gt = lfnt
        s = 1
        while s < td:
            gt = gt + jnp.where(lane_h >= s, pltpu.roll(gt, s, axis=1), 0.0)
            s *= 2
        g_ref[...] = jnp.broadcast_to(
            jnp.sum(jnp.where(lane == tok, rep_heads(gt), 0.0), axis=1, keepdims=True), (rows, PAGE))
        new_t = lambda ref: (lambda: pad_rows(ref[0]).T.astype(BF16))
        attend(new_t(kna_ref), new_t(vna_ref), -gt, new_t(kns_ref), new_t(vns_ref), lane <= tok, lane < tok, 1)

    def pages_t(refs):
        return lambda: jnp.concatenate([r[...].reshape(W_GRP, PAGE) for r in refs], axis=1).astype(BF16)

    r_parts = [None] * pg
    rc = rc_ref[...]
    for i in reversed(range(pg)):
        lf = lf_refs[i][...]
        x = lf
        s = 1
        while s < PAGE:
            x = x + jnp.where(lane_h + s < PAGE, pltpu.roll(x, PAGE - s, axis=1), 0.0)
            s *= 2
        r_parts[i] = (x - lf) + rc
        rc = rc + jnp.broadcast_to(x[:, 0:1], (N_HEADS, PAGE))
    rc_ref[...] = rc
    attend(pages_t(ka_refs), pages_t(va_refs), jnp.concatenate(r_parts, axis=1),
           pages_t(ks_refs), pages_t(vs_refs), None, None, pg)

    @pl.when(j == pl.num_programs(1) - 1)
    def _():
        inv = tile_lanes(1.0 / l_ref[...], W_GRP // PAGE)
        for acc, o_ref in ((acca_ref[...] * inv, oa_ref), (accs_ref[...], os_ref)):
            acc = jnp.where(own_cols, acc, 0.0)
            out = acc[0:td]
            for h in range(1, N_HEADS):
                out = out + acc[h * td:(h + 1) * td]
            o_ref[0] = out


def _decode_attn(layer, page_table, qa, qs, kna, vna, kns, vns, lfp, cka, cva, cks, cvs, clf):
    bd, n_pages = page_table.shape
    td = qa.shape[1]
    pg = PAGES_PER_STEP
    rows = N_HEADS * td
    small = lambda a: pl.BlockSpec((1,) + a.shape[1:], lambda b, j, pt: (b,) + (0,) * (a.ndim - 1))

    def page_spec(i, shape):
        def imap(b, j, pt):
            return (layer, pt[b, n_pages - (j + 1) * pg + i]) + (0,) * len(shape)
        return pl.BlockSpec((None, None) + shape, imap)

    in_specs = [small(a) for a in (qa, qs, kna, vna, kns, vns, lfp)]
    args = [qa, qs, kna, vna, kns, vns, lfp]
    for cache in (cka, cva, cks, cvs):
        in_specs += [page_spec(i, (N_HEADS, HEAD_DIM, PAGE)) for i in range(pg)]
        args += [cache] * pg
    in_specs += [page_spec(i, (N_HEADS, PAGE)) for i in range(pg)]
    args += [clf] * pg
    out_spec = pl.BlockSpec((1, td, W_GRP), lambda b, j, pt: (b, 0, 0))
    out_shape = jax.ShapeDtypeStruct((bd, td, W_GRP), F32)
    stat = pltpu.VMEM((rows, PAGE), F32)
    accum = pltpu.VMEM((rows, W_GRP), F32)
    qdiag = pltpu.VMEM((rows, W_GRP), BF16)
    grid_spec = pltpu.PrefetchScalarGridSpec(
        num_scalar_prefetch=1,
        grid=(bd, n_pages // pg),
        in_specs=in_specs,
        out_specs=[out_spec, out_spec],
        scratch_shapes=[qdiag, qdiag, stat, stat, accum, pltpu.VMEM((N_HEADS, PAGE), F32), stat, stat, accum],
    )
    return pl.pallas_call(
        functools.partial(_decode_kernel, pg=pg, td=td),
        grid_spec=grid_spec,
        out_shape=[out_shape, out_shape],
        compiler_params=_cparams(("arbitrary", "arbitrary")),
        name="decode_attn",
    )(page_table, *args)


def kernel(x_prompt, x_sample, cache_fox_k, cache_fox_v, cache_fox_logf, cache_sb_k, cache_sb_v,
           page_table, w_in, b_forget, w_out, ln_gain, ln_bias):
    batch, seq, d_model = x_prompt.shape
    bd, td, _ = x_sample.shape
    depth = w_in.shape[0]
    assert seq % T_TILE == 0 and page_table.shape[1] % PAGES_PER_STEP == 0
    assert cache_fox_k.shape[2:] == (PAGE, N_HEADS, HEAD_DIM) and w_in.shape[2] == 8 * W_GRP + N_HEADS
    nq = seq // T_TILE
    alpha = (2 * depth) ** 0.25

    ga_end = 4 * W_GRP
    w_main = jnp.concatenate([w_in[:, :, :ga_end], w_in[:, :, ga_end + N_HEADS:]], axis=2).astype(BF16)
    wf_pad = jnp.pad(w_in[:, :, ga_end:ga_end + N_HEADS], ((0, 0), (0, 0), (0, LANES - N_HEADS))).astype(BF16)
    bf_pad = jnp.pad(b_forget.astype(F32), ((0, 0), (0, LANES - N_HEADS)))[:, None, :]
    wo = w_out.astype(BF16)

    kv_view = lambda c: c.transpose(0, 1, 3, 4, 2)
    cka, cva, cks, cvs = (kv_view(c) for c in (cache_fox_k, cache_fox_v, cache_sb_k, cache_sb_v))
    clf = jnp.swapaxes(cache_fox_logf, 2, 3)

    xp = x_prompt.reshape(batch * seq, d_model)
    xs = x_sample.reshape(bd * td, d_model)
    bufs = None
    rows_s = []
    per_b = lambda a: a.reshape(bd, td, a.shape[-1])
    for l in range(depth):
        lg, lb = ln_gain[l][None, :], ln_bias[l][None, :]
        outs = _proj_prompt(l, xp, w_main[l], wf_pad[l], bf_pad[l], bufs, depth, batch, nq)
        bufs = outs[:5]
        kab, ksb, qat, gat, qst, gst = outs[5:]
        dt = _dcum(l, bufs[4], nq)
        oat = _attn_prompt("fox", l, qat, kab, bufs[1], dt, batch, nq)
        ost = _attn_prompt("sb", l, qst, ksb, bufs[3], None, batch, nq)
        xp = _outproj(oat, ost, gat, gst, xp, wo[l], lg, lb, alpha, True)
        (qa_s, ka_s, va_s, ga_s, qs_s, ks_s, vs_s, gs_s, lf_s, lfp_s) = _proj_sample(
            xs, w_main[l], wf_pad[l], bf_pad[l])
        oa_s, os_s = _decode_attn(l, page_table, per_b(qa_s), per_b(qs_s), per_b(ka_s), per_b(va_s),
                                  per_b(ks_s), per_b(vs_s), per_b(lfp_s), cka, cva, cks, cvs, clf)
        xs = _outproj(oa_s.reshape(bd * td, W_GRP), os_s.reshape(bd * td, W_GRP), ga_s, gs_s, xs,
                      wo[l], lg, lb, alpha, False)
        rows_s.append((ka_s, va_s, lf_s, ks_s, vs_s))

    kv_out = lambda b: b.reshape(depth, batch, N_HEADS, HEAD_DIM, seq).transpose(0, 1, 4, 2, 3)
    lf_out = bufs[4].transpose(0, 1, 3, 2)

    def stack_s(idx, shape):
        return jnp.stack([r[idx] for r in rows_s]).reshape((depth,) + shape)

    kv_s, lf_sh = (bd, td, N_HEADS, HEAD_DIM), (bd, td, N_HEADS)
    return (xp.reshape(batch, seq, d_model), xs.reshape(bd, td, d_model),
            kv_out(bufs[0]), kv_out(bufs[1]), lf_out, kv_out(bufs[2]), kv_out(bufs[3]),
            stack_s(0, kv_s), stack_s(1, kv_s), stack_s(2, lf_sh), stack_s(3, kv_s), stack_s(4, kv_s))
```

```python
import functools

import jax
import jax.numpy as jnp
from jax import lax
from jax.experimental import pallas as pl
from jax.experimental.pallas import tpu as pltpu

F32 = jnp.float32
BF16 = jnp.bfloat16

HEAD_DIM = 64
N_HEADS = 8
W_GRP = N_HEADS * HEAD_DIM
PAGE = 128
LANES = 128
LN_EPS = 1e-5
NEG = -1e30
VMEM_LIMIT = 56 * 1024 * 1024
QK_SCALE = HEAD_DIM ** -0.5

T_TILE = 512
K_SUB = 256
K_SUB_FOX = 512
FOX_PAIRS = 4
SB_PAIRS = 4
PAGES_PER_STEP = 16


def _log_sigmoid(x):
    return jnp.minimum(x, 0.0) - jnp.log(1.0 + jnp.exp(-jnp.abs(x)))


def _neg_abs(x):
    bits = lax.bitcast_convert_type(x, jnp.uint32) | jnp.uint32(0x80000000)
    return lax.bitcast_convert_type(bits, F32)


def _softplus(x):
    return jnp.maximum(x, 0.0) + jnp.log(1.0 + jnp.exp(_neg_abs(x)))


def _silu(g):
    return g * (1.0 / (1.0 + jnp.exp(-g)))


def _cparams(sem):
    return pltpu.CompilerParams(dimension_semantics=sem, vmem_limit_bytes=VMEM_LIMIT)


def _pick_row(x, idx):
    sub = lax.broadcasted_iota(jnp.int32, x.shape, 0)
    return jnp.sum(jnp.where(sub == idx, x, 0.0), axis=0, keepdims=True)


def _proj_prompt_kernel(x_ref, w_ref, wf_ref, bf_ref, *rest):
    (kat_ref, vat_ref, kst_ref, vst_ref, lft_ref,
     kab_ref, ksb_ref, qat_ref, gat_ref, qst_ref, gst_ref) = rest[-11:]
    xb = x_ref[...].astype(BF16)

    def chunk(c):
        return jnp.dot(xb, w_ref[:, c * W_GRP:(c + 1) * W_GRP], preferred_element_type=F32)

    qat_ref[0] = (chunk(0) * QK_SCALE).T.astype(BF16)
    ka = chunk(1)
    kab_ref[...] = ka.astype(BF16)
    kat_ref[...] = ka.T
    vat_ref[...] = chunk(2).T
    gat_ref[0] = chunk(3).T.astype(BF16)
    qst_ref[0] = (chunk(4) * QK_SCALE).T.astype(BF16)
    ks = chunk(5)
    ksb_ref[...] = ks.astype(BF16)
    kst_ref[...] = ks.T
    vst_ref[...] = chunk(6).T
    gst_ref[0] = chunk(7).T.astype(BF16)
    fa = jnp.dot(xb, wf_ref[...], preferred_element_type=F32) + bf_ref[...]
    lft_ref[...] = _log_sigmoid(fa).T[:N_HEADS, :]


def _proj_prompt(layer, x, w, wf, bfp, bufs, depth, batch, nq):
    m, d = x.shape
    tm = T_TILE
    nt = m // tm
    seq = nq * tm
    kv_buf = jax.ShapeDtypeStruct((depth, batch, W_GRP, seq), F32)
    lf_buf = jax.ShapeDtypeStruct((depth, batch, N_HEADS, seq), F32)
    buf_spec = lambda r: pl.BlockSpec((None, None, r, tm), lambda i: (layer, i // nq, 0, i % nq))
    tok = jax.ShapeDtypeStruct((m, W_GRP), BF16)
    tok_spec = pl.BlockSpec((tm, W_GRP), lambda i: (i, 0))
    til = jax.ShapeDtypeStruct((nt, W_GRP, tm), BF16)
    til_spec = pl.BlockSpec((1, W_GRP, tm), lambda i: (i, 0, 0))
    out_shape = [kv_buf] * 4 + [lf_buf, tok, tok] + [til] * 4
    out_specs = [buf_spec(W_GRP)] * 4 + [buf_spec(N_HEADS), tok_spec, tok_spec] + [til_spec] * 4
    in_specs = [pl.BlockSpec((tm, d), lambda i: (i, 0)),
                pl.BlockSpec(w.shape, lambda i: (0, 0)),
                pl.BlockSpec(wf.shape, lambda i: (0, 0)),
                pl.BlockSpec(bfp.shape, lambda i: (0, 0))]
    args = [x, w, wf, bfp]
    aliases = {}
    if bufs is not None:
        in_specs += [pl.BlockSpec(memory_space=pl.ANY)] * 5
        args += list(bufs)
        aliases = {4 + k: k for k in range(5)}
    return pl.pallas_call(
        _proj_prompt_kernel,
        grid=(nt,),
        in_specs=in_specs,
        out_specs=out_specs,
        out_shape=out_shape,
        input_output_aliases=aliases,
        compiler_params=_cparams(("arbitrary",)),
        name="proj_prompt",
    )(*args)


def _proj_sample_kernel(x_ref, w_ref, wf_ref, bf_ref,
                        qa_ref, ka_ref, va_ref, ga_ref, qs_ref, ks_ref, vs_ref, gs_ref, lf_ref, lfp_ref):
    xb = x_ref[...].astype(BF16)
    outs = (qa_ref, ka_ref, va_ref, ga_ref, qs_ref, ks_ref, vs_ref, gs_ref)
    for c, o_ref in enumerate(outs):
        r = jnp.dot(xb, w_ref[:, c * W_GRP:(c + 1) * W_GRP], preferred_element_type=F32)
        o_ref[...] = r * QK_SCALE if c % 4 == 0 else r
    fa = jnp.dot(xb, wf_ref[...], preferred_element_type=F32) + bf_ref[...]
    lf = _log_sigmoid(fa)
    lf_ref[...] = lf[:, :N_HEADS]
    lfp_ref[...] = lf


def _proj_sample(x, w, wf, bfp):
    m, _ = x.shape
    full = lambda a: pl.BlockSpec(a.shape, lambda i: (0,) * a.ndim)
    shapes = [(m, W_GRP)] * 8 + [(m, N_HEADS), (m, LANES)]
    return pl.pallas_call(
        _proj_sample_kernel,
        grid=(1,),
        in_specs=[full(x), full(w), full(wf), full(bfp)],
        out_specs=[pl.BlockSpec(s, lambda i: (0, 0)) for s in shapes],
        out_shape=[jax.ShapeDtypeStruct(s, F32) for s in shapes],
        compiler_params=_cparams(("arbitrary",)),
        name="proj_sample",
    )(x, w, wf, bfp)


def _dcum_kernel(lft_ref, dt_ref, *, nq):
    x = lft_ref[...]
    t = x.shape[1]
    lane = lax.broadcasted_iota(jnp.int32, x.shape, 1)
    s = 1
    while s < t:
        x = x + jnp.where(lane >= s, pltpu.roll(x, s, axis=1), 0.0)
        s *= 2
    for i in range(nq):
        dt_ref[i] = x[:, i * T_TILE:(i + 1) * T_TILE]


def _dcum(layer, lf_buf, nq):
    _, batch, _, seq = lf_buf.shape
    return pl.pallas_call(
        functools.partial(_dcum_kernel, nq=nq),
        grid=(batch,),
        in_specs=[pl.BlockSpec((None, None, N_HEADS, seq), lambda b: (layer, b, 0, 0))],
        out_specs=pl.BlockSpec((nq, N_HEADS, T_TILE), lambda b: (b, 0, 0)),
        out_shape=jax.ShapeDtypeStruct((batch * nq, N_HEADS, T_TILE), F32),
        compiler_params=_cparams(("arbitrary",)),
        name="forget_cumsum",
    )(lf_buf)


def _pair_q(q):
    top = lax.broadcasted_iota(jnp.int32, q.shape, 0) < HEAD_DIM
    zero = jnp.zeros_like(q)
    return jnp.concatenate([jnp.where(top, q, zero), jnp.where(top, zero, q)], axis=1)


def _fox_prompt_kernel(q_ref, k_ref, v_ref, d_ref, o_ref, dk_ref, *, nq):
    tq, tk = T_TILE, K_SUB_FOX
    nsub, nrep, w = tq // tk, tq // LANES, 2 * tq
    pr = 2 * HEAD_DIM
    npair = q_ref.shape[1] // pr
    first = pl.program_id(1) * 2 * npair
    for kt in range(nq):
        for c in range(nrep):
            tile_d = d_ref[kt, :, c * LANES:(c + 1) * LANES]
            for e in range(2 * npair):
                row = _pick_row(tile_d, first + e)
                dk_ref[kt * tq + c * LANES:kt * tq + (c + 1) * LANES, e * LANES:(e + 1) * LANES] = (
                    jnp.broadcast_to(row, (LANES, LANES)).T)
    row_i = lax.broadcasted_iota(jnp.int32, (tk, w), 0)
    col_i = lax.broadcasted_iota(jnp.int32, (tk, w), 1) % tq

    def tile(kt, carry, qms, dqs, masks):
        ms, ls, accs = list(carry[0]), list(carry[1]), list(carry[2])
        r0s = [pl.multiple_of(kt * tq + jj * tk, tk) for jj in range(nsub)]
        ss = [[None] * nsub for _ in range(npair)]
        for p in range(npair):
            for jj, r0 in enumerate(r0s):
                s = jnp.dot(k_ref[pl.ds(r0, tk), p * pr:(p + 1) * pr], qms[p],
                            preferred_element_type=F32)
                dk = dk_ref[pl.ds(r0, tk), 2 * p * LANES:(2 * p + 2) * LANES]
                s = s + (dqs[p] - jnp.concatenate([dk[:, :LANES]] * nrep + [dk[:, LANES:]] * nrep, axis=1))
                if masks is not None:
                    s = jnp.where(masks[jj], s, NEG)
                ss[p][jj] = s
        for p in range(npair):
            m_new = ms[p]
            for s in ss[p]:
                m_new = jnp.maximum(m_new, jnp.max(s, axis=0, keepdims=True))
            alpha = jnp.exp(ms[p] - m_new)
            l = alpha * ls[p]
            acc_e = alpha[:, :tq] * accs[2 * p]
            acc_o = alpha[:, tq:] * accs[2 * p + 1]
            for s, r0 in zip(ss[p], r0s):
                pw = jnp.exp(s - m_new)
                l = l + jnp.sum(pw, axis=0, keepdims=True)
                pb = pw.astype(BF16)
                vb = v_ref[p * pr:(p + 1) * pr, pl.ds(r0, tk)].astype(BF16)
                acc_e = acc_e + jnp.dot(vb[:HEAD_DIM], pb[:, :tq], preferred_element_type=F32)
                acc_o = acc_o + jnp.dot(vb[HEAD_DIM:], pb[:, tq:], preferred_element_type=F32)
            ms[p], ls[p], accs[2 * p], accs[2 * p + 1] = m_new, l, acc_e, acc_o
        return tuple(ms), tuple(ls), tuple(accs)

    def q_tile(qi, _):
        q_all = q_ref[qi]
        qms = [_pair_q(q_all[p * pr:(p + 1) * pr]) for p in range(npair)]
        dqs = [jnp.concatenate([_pick_row(d_ref[qi], first + 2 * p), _pick_row(d_ref[qi], first + 2 * p + 1)],
                               axis=1) for p in range(npair)]
        carry = (tuple(jnp.full((1, w), NEG, F32) for _ in range(npair)),
                 tuple(jnp.zeros((1, w), F32) for _ in range(npair)),
                 tuple(jnp.zeros((HEAD_DIM, tq), F32) for _ in range(2 * npair)))
        carry = lax.fori_loop(0, qi, lambda kt, c: tile(kt, c, qms, dqs, None), carry)
        carry = tile(qi, carry, qms, dqs, [(row_i + jj * tk) <= col_i for jj in range(nsub)])
        outs = []
        for p in range(npair):
            inv = 1.0 / carry[1][p]
            outs += [carry[2][2 * p] * inv[:, :tq], carry[2][2 * p + 1] * inv[:, tq:]]
        o_ref[qi] = jnp.concatenate(outs, axis=0).astype(BF16)
        return 0

    lax.fori_loop(0, nq, q_tile, 0)


def _sb_prompt_kernel(q_ref, k_ref, v_ref, o_ref, *, nq):
    tq, tk = T_TILE, K_SUB
    nsub, w = tq // tk, 2 * tq
    row_i = lax.broadcasted_iota(jnp.int32, (tk, w), 0)
    col_i = lax.broadcasted_iota(jnp.int32, (tk, w), 1) % tq
    ui = lax.broadcasted_iota(jnp.int32, (tk, tk), 0)
    uj = lax.broadcasted_iota(jnp.int32, (tk, tk), 1)
    upper = jnp.where(uj >= ui, 1.0, 0.0).astype(BF16)

    pr = 2 * HEAD_DIM
    npair = q_ref.shape[1] // pr

    def tile(kt, carry, qms, masks):
        r0s = [pl.multiple_of(kt * tq + jj * tk, tk) for jj in reversed(range(nsub))]
        units = [(p, i) for p in range(npair) for i in range(nsub)]
        zs = [jnp.dot(k_ref[pl.ds(r0s[i], tk), p * pr:(p + 1) * pr], qms[p], preferred_element_type=F32)
              for p, i in units]
        sps = [_softplus(z) for z in zs]
        if masks is not None:
            sps = [jnp.where(masks[i], sp, 0.0) for (p, i), sp in zip(units, sps)]
        css = [jnp.dot(upper, sp.astype(BF16), preferred_element_type=F32) for sp in sps]
        as_ = [jnp.exp(z - cs) for z, cs in zip(zs, css)]
        if masks is not None:
            as_ = [jnp.where(masks[i], a, 0.0) for (p, i), a in zip(units, as_)]
        tots, accs = list(carry[0]), list(carry[1])
        for (p, i), a, cs in zip(units, as_, css):
            ab = a.astype(BF16)
            later = jnp.exp(-tots[p])
            vb = v_ref[p * pr:(p + 1) * pr, pl.ds(r0s[i], tk)].astype(BF16)
            accs[2 * p] = accs[2 * p] + later[:, :tq] * jnp.dot(
                vb[:HEAD_DIM], ab[:, :tq], preferred_element_type=F32)
            accs[2 * p + 1] = accs[2 * p + 1] + later[:, tq:] * jnp.dot(
                vb[HEAD_DIM:], ab[:, tq:], preferred_element_type=F32)
            tots[p] = tots[p] + cs[0:1, :]
        return tuple(tots), tuple(accs)

    def q_tile(qi, _):
        q_all = q_ref[qi]
        qms = [_pair_q(q_all[p * pr:(p + 1) * pr]) for p in range(npair)]
        carry = (tuple(jnp.zeros((1, w), F32) for _ in range(npair)),
                 tuple(jnp.zeros((HEAD_DIM, tq), F32) for _ in range(2 * npair)))
        carry = tile(qi, carry, qms, [(row_i + jj * tk) < col_i for jj in reversed(range(nsub))])
        carry = lax.fori_loop(0, qi, lambda it, c: tile(qi - 1 - it, c, qms, None), carry)
        o_ref[qi] = jnp.concatenate(carry[1], axis=0).astype(BF16)
        return 0

    lax.fori_loop(0, nq, q_tile, 0)


def _attn_prompt(kind, layer, qt, kb, vbuf, dt, batch, nq):
    t = nq * T_TILE
    pair_rows = 2 * HEAD_DIM * (FOX_PAIRS if kind == "fox" else SB_PAIRS)
    til = pl.BlockSpec((nq, pair_rows, T_TILE), lambda b, p: (b, p, 0))
    in_specs = [til,
                pl.BlockSpec((t, pair_rows), lambda b, p: (b, p)),
                pl.BlockSpec((None, None, pair_rows, t), lambda b, p: (layer, b, p, 0))]
    args = [qt, kb, vbuf]
    scratch = []
    if kind == "fox":
        in_specs.append(pl.BlockSpec((nq, N_HEADS, T_TILE), lambda b, p: (b, 0, 0)))
        args.append(dt)
        scratch.append(pltpu.VMEM((t, 2 * LANES * FOX_PAIRS), F32))
        body = functools.partial(_fox_prompt_kernel, nq=nq)
    else:
        body = functools.partial(_sb_prompt_kernel, nq=nq)
    return pl.pallas_call(
        body,
        grid=(batch, W_GRP // pair_rows),
        in_specs=in_specs,
        out_specs=til,
        out_shape=jax.ShapeDtypeStruct(qt.shape, BF16),
        scratch_shapes=scratch,
        compiler_params=_cparams(("arbitrary", "arbitrary")),
        name=kind + "_attn_prompt",
    )(*args)


def _outproj_kernel(oa_ref, os_ref, ga_ref, gs_ref, x_ref, wo_ref, lg_ref, lb_ref, y_ref, *, alpha, transposed):
    if transposed:
        ua = (oa_ref[0].astype(F32) * _silu(ga_ref[0].astype(F32))).T.astype(BF16)
        us = (os_ref[0].astype(F32) * _silu(gs_ref[0].astype(F32))).T.astype(BF16)
    else:
        ua = (oa_ref[...] * _silu(ga_ref[...])).astype(BF16)
        us = (os_ref[...] * _silu(gs_ref[...])).astype(BF16)
    y = (jnp.dot(ua, wo_ref[:W_GRP, :], preferred_element_type=F32)
         + jnp.dot(us, wo_ref[W_GRP:, :], preferred_element_type=F32))
    r = alpha * x_ref[...] + y
    mu = jnp.mean(r, axis=-1, keepdims=True)
    c = r - mu
    var = jnp.mean(c * c, axis=-1, keepdims=True)
    y_ref[...] = c * lax.rsqrt(var + LN_EPS) * lg_ref[...] + lb_ref[...]


def _outproj(oa, os_, ga, gs, x, wo, lg, lb, alpha, transposed):
    m, d = x.shape
    if transposed:
        tm = T_TILE
        o_spec = pl.BlockSpec((1, W_GRP, tm), lambda i: (i, 0, 0))
    else:
        tm = m
        o_spec = pl.BlockSpec((tm, W_GRP), lambda i: (i, 0))
    vec = pl.BlockSpec((1, d), lambda i: (0, 0))
    return pl.pallas_call(
        functools.partial(_outproj_kernel, alpha=alpha, transposed=transposed),
        grid=(m // tm,),
        in_specs=[o_spec] * 4 + [pl.BlockSpec((tm, d), lambda i: (i, 0)),
                                 pl.BlockSpec(wo.shape, lambda i: (0, 0)), vec, vec],
        out_specs=pl.BlockSpec((tm, d), lambda i: (i, 0)),
        out_shape=jax.ShapeDtypeStruct((m, d), F32),
        compiler_params=_cparams(("arbitrary",)),
        name="outproj_ln_" + ("prompt" if transposed else "sample"),
    )(oa, os_, ga, gs, x, wo, lg, lb)


def _decode_kernel(pt_ref, qa_ref, qs_ref, kna_ref, vna_ref, kns_ref, vns_ref, lfp_ref, *rest, pg, td):
    del pt_ref
    ka_refs, va_refs = rest[0:pg], rest[pg:2 * pg]
    ks_refs, vs_refs = rest[2 * pg:3 * pg], rest[3 * pg:4 * pg]
    lf_refs = rest[4 * pg:5 * pg]
    oa_ref, os_ref = rest[5 * pg], rest[5 * pg + 1]
    qda_ref, qds_ref, m_ref, l_ref, acca_ref, rc_ref, g_ref, tot_ref, accs_ref = rest[5 * pg + 2:]
    rows = N_HEADS * td
    j = pl.program_id(1)
    lane = lax.broadcasted_iota(jnp.int32, (rows, PAGE), 1)
    tok = lax.broadcasted_iota(jnp.int32, (rows, PAGE), 0) % td
    lane_h = lax.broadcasted_iota(jnp.int32, (N_HEADS, PAGE), 1)
    ui = lax.broadcasted_iota(jnp.int32, (PAGE, PAGE), 0)
    uj = lax.broadcasted_iota(jnp.int32, (PAGE, PAGE), 1)
    later = jnp.where(ui >= uj, 1.0, 0.0).astype(BF16)
    own_cols = (lax.broadcasted_iota(jnp.int32, (rows, W_GRP), 1) // HEAD_DIM
                == lax.broadcasted_iota(jnp.int32, (rows, W_GRP), 0) // td)

    def rep_heads(r):
        return jnp.concatenate(
            [jnp.broadcast_to(r[h:h + 1, :], (td, r.shape[1])) for h in range(N_HEADS)], axis=0)

    def tile_lanes(x, n):
        return x if n == 1 else jnp.concatenate([x] * n, axis=1)

    def nt_dot(p, vt):
        return lax.dot_general(p.astype(BF16), vt, (((1,), (1,)), ((), ())), preferred_element_type=F32)

    def attend(kt_a, vt_a, r_keys, kt_s, vt_s, mask_a, mask_s, n):
        s = jnp.dot(qda_ref[...], kt_a(), preferred_element_type=F32)
        z = jnp.dot(qds_ref[...], kt_s(), preferred_element_type=F32)
        s = s + tile_lanes(g_ref[...], n) + rep_heads(r_keys)
        if mask_a is not None:
            s = jnp.where(mask_a, s, NEG)
        sp = _softplus(z)
        if mask_s is not None:
            sp = jnp.where(mask_s, sp, 0.0)
        stack = sp if n == 1 else jnp.concatenate([sp[:, i * PAGE:(i + 1) * PAGE] for i in range(n)], axis=0)
        cs = jnp.dot(stack.astype(BF16), later, preferred_element_type=F32)
        m_prev = m_ref[...]
        m_new = jnp.maximum(m_prev, jnp.max(s, axis=1, keepdims=True))
        alpha = jnp.exp(m_prev - m_new)
        p = jnp.exp(s - tile_lanes(m_new, n))
        l_ref[...] = alpha * l_ref[...] + jnp.sum(p, axis=1, keepdims=True)
        m_ref[...] = m_new
        carry = tot_ref[...]
        parts = [None] * n
        for i in reversed(range(n)):
            cs_i = cs[i * rows:(i + 1) * rows, :]
            parts[i] = jnp.exp(z[:, i * PAGE:(i + 1) * PAGE] - cs_i - carry)
            carry = carry + jnp.broadcast_to(cs_i[:, 0:1], (rows, PAGE))
        tot_ref[...] = carry
        a = parts[0] if n == 1 else jnp.concatenate(parts, axis=1)
        if mask_s is not None:
            a = jnp.where(mask_s, a, 0.0)
        acca_ref[...] = tile_lanes(alpha, W_GRP // PAGE) * acca_ref[...] + nt_dot(p, vt_a())
        accs_ref[...] = accs_ref[...] + nt_dot(a, vt_s())

    @pl.when(j == 0)
    def _():
        m_ref[...] = jnp.full(m_ref.shape, NEG, F32)
        l_ref[...] = jnp.zeros(l_ref.shape, F32)
        acca_ref[...] = jnp.zeros(acca_ref.shape, F32)
        rc_ref[...] = jnp.zeros(rc_ref.shape, F32)
        tot_ref[...] = jnp.zeros(tot_ref.shape, F32)
        accs_ref[...] = jnp.zeros(accs_ref.shape, F32)
        for q_ref, qd_ref in ((qa_ref, qda_ref), (qs_ref, qds_ref)):
            q8 = jnp.concatenate([q_ref[0]] * N_HEADS, axis=0)
            qd_ref[...] = jnp.where(own_cols, q8, 0.0).astype(BF16)
        pad_rows = lambda x: jnp.concatenate([x, jnp.zeros((PAGE - td, x.shape[1]), F32)], axis=0)
        lfnt = pad_rows(lfp_ref[0]).T[:N_HEADS, :]
        gt = lfnt
        s = 1
        while s < td:
            gt = gt + jnp.where(lane_h >= s, pltpu.roll(gt, s, axis=1), 0.0)
            s *= 2
        g_ref[...] = jnp.broadcast_to(
            jnp.sum(jnp.where(lane == tok, rep_heads(gt), 0.0), axis=1, keepdims=True), (rows, PAGE))
        new_t = lambda ref: (lambda: pad_rows(ref[0]).T.astype(BF16))
        attend(new_t(kna_ref), new_t(vna_ref), -gt, new_t(kns_ref), new_t(vns_ref), lane <= tok, lane < tok, 1)

    def pages_t(refs):
        return lambda: jnp.concatenate([r[...].reshape(W_GRP, PAGE) for r in refs], axis=1).astype(BF16)

    r_parts = [None] * pg
    rc = rc_ref[...]
    for i in reversed(range(pg)):
        lf = lf_refs[i][...]
        x = lf
        s = 1
        while s < PAGE:
            x = x + jnp.where(lane_h + s < PAGE, pltpu.roll(x, PAGE - s, axis=1), 0.0)
            s *= 2
        r_parts[i] = (x - lf) + rc
        rc = rc + jnp.broadcast_to(x[:, 0:1], (N_HEADS, PAGE))
    rc_ref[...] = rc
    attend(pages_t(ka_refs), pages_t(va_refs), jnp.concatenate(r_parts, axis=1),
           pages_t(ks_refs), pages_t(vs_refs), None, None, pg)

    @pl.when(j == pl.num_programs(1) - 1)
    def _():
        inv = tile_lanes(1.0 / l_ref[...], W_GRP // PAGE)
        for acc, o_ref in ((acca_ref[...] * inv, oa_ref), (accs_ref[...], os_ref)):
            acc = jnp.where(own_cols, acc, 0.0)
            out = acc[0:td]
            for h in range(1, N_HEADS):
                out = out + acc[h * td:(h + 1) * td]
            o_ref[0] = out


def _decode_attn(layer, page_table, qa, qs, kna, vna, kns, vns, lfp, cka, cva, cks, cvs, clf):
    bd, n_pages = page_table.shape
    td = qa.shape[1]
    pg = PAGES_PER_STEP
    rows = N_HEADS * td
    small = lambda a: pl.BlockSpec((1,) + a.shape[1:], lambda b, j, pt: (b,) + (0,) * (a.ndim - 1))

    def page_spec(i, shape):
        def imap(b, j, pt):
            return (layer, pt[b, n_pages - (j + 1) * pg + i]) + (0,) * len(shape)
        return pl.BlockSpec((None, None) + shape, imap)

    in_specs = [small(a) for a in (qa, qs, kna, vna, kns, vns, lfp)]
    args = [qa, qs, kna, vna, kns, vns, lfp]
    for cache in (cka, cva, cks, cvs):
        in_specs += [page_spec(i, (N_HEADS, HEAD_DIM, PAGE)) for i in range(pg)]
        args += [cache] * pg
    in_specs += [page_spec(i, (N_HEADS, PAGE)) for i in range(pg)]
    args += [clf] * pg
    out_spec = pl.BlockSpec((1, td, W_GRP), lambda b, j, pt: (b, 0, 0))
    out_shape = jax.ShapeDtypeStruct((bd, td, W_GRP), F32)
    stat = pltpu.VMEM((rows, PAGE), F32)
    accum = pltpu.VMEM((rows, W_GRP), F32)
    qdiag = pltpu.VMEM((rows, W_GRP), BF16)
    grid_spec = pltpu.PrefetchScalarGridSpec(
        num_scalar_prefetch=1,
        grid=(bd, n_pages // pg),
        in_specs=in_specs,
        out_specs=[out_spec, out_spec],
        scratch_shapes=[qdiag, qdiag, stat, stat, accum, pltpu.VMEM((N_HEADS, PAGE), F32), stat, stat, accum],
    )
    return pl.pallas_call(
        functools.partial(_decode_kernel, pg=pg, td=td),
        grid_spec=grid_spec,
        out_shape=[out_shape, out_shape],
        compiler_params=_cparams(("arbitrary", "arbitrary")),
        name="decode_attn",
    )(page_table, *args)


def kernel(x_prompt, x_sample, cache_fox_k, cache_fox_v, cache_fox_logf, cache_sb_k, cache_sb_v,
           page_table, w_in, b_forget, w_out, ln_gain, ln_bias):
    batch, seq, d_model = x_prompt.shape
    bd, td, _ = x_sample.shape
    depth = w_in.shape[0]
    assert seq % T_TILE == 0 and page_table.shape[1] % PAGES_PER_STEP == 0
    assert cache_fox_k.shape[2:] == (PAGE, N_HEADS, HEAD_DIM) and w_in.shape[2] == 8 * W_GRP + N_HEADS
    nq = seq // T_TILE
    alpha = (2 * depth) ** 0.25

    ga_end = 4 * W_GRP
    w_main = jnp.concatenate([w_in[:, :, :ga_end], w_in[:, :, ga_end + N_HEADS:]], axis=2).astype(BF16)
    wf_pad = jnp.pad(w_in[:, :, ga_end:ga_end + N_HEADS], ((0, 0), (0, 0), (0, LANES - N_HEADS))).astype(BF16)
    bf_pad = jnp.pad(b_forget.astype(F32), ((0, 0), (0, LANES - N_HEADS)))[:, None, :]
    wo = w_out.astype(BF16)

    kv_view = lambda c: c.transpose(0, 1, 3, 4, 2)
    cka, cva, cks, cvs = (kv_view(c) for c in (cache_fox_k, cache_fox_v, cache_sb_k, cache_sb_v))
    clf = jnp.swapaxes(cache_fox_logf, 2, 3)

    xp = x_prompt.reshape(batch * seq, d_model)
    xs = x_sample.reshape(bd * td, d_model)
    bufs = None
    rows_s = []
    per_b = lambda a: a.reshape(bd, td, a.shape[-1])
    for l in range(depth):
        lg, lb = ln_gain[l][None, :], ln_bias[l][None, :]
        outs = _proj_prompt(l, xp, w_main[l], wf_pad[l], bf_pad[l], bufs, depth, batch, nq)
        bufs = outs[:5]
        kab, ksb, qat, gat, qst, gst = outs[5:]
        dt = _dcum(l, bufs[4], nq)
        oat = _attn_prompt("fox", l, qat, kab, bufs[1], dt, batch, nq)
        ost = _attn_prompt("sb", l, qst, ksb, bufs[3], None, batch, nq)
        xp = _outproj(oat, ost, gat, gst, xp, wo[l], lg, lb, alpha, True)
        (qa_s, ka_s, va_s, ga_s, qs_s, ks_s, vs_s, gs_s, lf_s, lfp_s) = _proj_sample(
            xs, w_main[l], wf_pad[l], bf_pad[l])
        oa_s, os_s = _decode_attn(l, page_table, per_b(qa_s), per_b(qs_s), per_b(ka_s), per_b(va_s),
                                  per_b(ks_s), per_b(vs_s), per_b(lfp_s), cka, cva, cks, cvs, clf)
        xs = _outproj(oa_s.reshape(bd * td, W_GRP), os_s.reshape(bd * td, W_GRP), ga_s, gs_s, xs,
                      wo[l], lg, lb, alpha, False)
        rows_s.append((ka_s, va_s, lf_s, ks_s, vs_s))

    kv_out = lambda b: b.reshape(depth, batch, N_HEADS, HEAD_DIM, seq).transpose(0, 1, 4, 2, 3)
    lf_out = bufs[4].transpose(0, 1, 3, 2)

    def stack_s(idx, shape):
        return jnp.stack([r[idx] for r in rows_s]).reshape((depth,) + shape)

    kv_s, lf_sh = (bd, td, N_HEADS, HEAD_DIM), (bd, td, N_HEADS)
    return (xp.reshape(batch, seq, d_model), xs.reshape(bd, td, d_model),
            kv_out(bufs[0]), kv_out(bufs[1]), lf_out, kv_out(bufs[2]), kv_out(bufs[3]),
            stack_s(0, kv_s), stack_s(1, kv_s), stack_s(2, lf_sh), stack_s(3, kv_s), stack_s(4, kv_s))
```
